```python
import math
import jax, jax.numpy as jnp
from jax import lax
import numpy as np

D_MODEL = 1024
BATCH = 8
SEQ = 4096
DEPTH = 2

N_EVEN = (DEPTH + 1) // 2
N_ODD = DEPTH // 2

GRID_W = 64

MLA_HEADS = 8
MLA_Q_LORA = 256
MLA_KV_LORA = 128
MLA_NOPE = 64
MLA_ROPE = 32
MLA_V = 64
MLA_QBLOCK = 128
ROPE_THETA = 10000.0

NA_HEADS = 8
NA_HEAD_DIM = 64
NA_WIN_H = 8
NA_WIN_W = 16
NA_WIDTH = NA_HEADS * NA_HEAD_DIM

RET_HEADS = 4
RET_QK_DIM = 128
RET_V_DIM = 128
RET_CHUNK = 128
RET_QK_W = RET_HEADS * RET_QK_DIM
RET_V_W = RET_HEADS * RET_V_DIM

HY_WIDTH = 512
HY_EMB_DIM = 33
HY_FILTER_HIDDEN = 64
HY_FAST_DECAY = 0.3
HY_SLOW_DECAY = 1.5
HY_TARGET = 1e-2

D_FF = 2816
SHORT_CONV = 3
EPS = 1e-6

A_IN = MLA_Q_LORA + MLA_KV_LORA + MLA_ROPE + 3 * NA_WIDTH
A_OUT = MLA_HEADS * MLA_V + NA_WIDTH
C_IN = 2 * RET_QK_W + 2 * RET_V_W + 3 * HY_WIDTH
C_OUT = RET_V_W + HY_WIDTH

kernel_name = 'hybrid_mla_natten_retnet_hyena_encoder'


def rms_norm(x, gain):
    xf = x.astype(jnp.float32)
    y = xf * lax.rsqrt(jnp.mean(xf * xf, axis=-1, keepdims=True) + EPS)
    return (y * gain.astype(jnp.float32)).astype(x.dtype)


def dwconv3(x, w):
    xp = jnp.pad(x, ((0, 0), (1, 1), (0, 0)))
    return xp[:, :-2] * w[0] + xp[:, 1:-1] * w[1] + xp[:, 2:] * w[2]


def split_cols(t, sizes):
    out, off = [], 0
    for s in sizes:
        out.append(t[..., off:off + s])
        off += s
    return out


def rope_tables(length, dim):
    inv = ROPE_THETA ** (-jnp.arange(0, dim, 2, dtype=jnp.float32) / dim)
    ang = jnp.arange(length, dtype=jnp.float32)[:, None] * inv[None, :]
    return jnp.cos(ang), jnp.sin(ang)


def apply_rope(x, cos, sin):
    shape = (1, cos.shape[0]) + (1,) * (x.ndim - 3) + (cos.shape[1],)
    c = cos.reshape(shape).astype(x.dtype)
    s = sin.reshape(shape).astype(x.dtype)
    x1, x2 = jnp.split(x, 2, axis=-1)
    return jnp.concatenate([x1 * c - x2 * s, x1 * s + x2 * c], axis=-1)


def mla_attention(cq, ckv, k_rope, q_norm, w_q_up, kv_norm, w_kv_up):
    B, S, _ = cq.shape
    q = (rms_norm(cq, q_norm) @ w_q_up).reshape(B, S, MLA_HEADS, MLA_NOPE + MLA_ROPE)
    kv = (rms_norm(ckv, kv_norm) @ w_kv_up).reshape(B, S, MLA_HEADS, MLA_NOPE + MLA_V)
    q_nope, q_rope = q[..., :MLA_NOPE], q[..., MLA_NOPE:]
    k_nope, v = kv[..., :MLA_NOPE], kv[..., MLA_NOPE:]
    cos, sin = rope_tables(S, MLA_ROPE)
    q_rope = apply_rope(q_rope, cos, sin)
    k_rope = apply_rope(k_rope, cos, sin)
    scale = (MLA_NOPE + MLA_ROPE) ** -0.5
    nb = S // MLA_QBLOCK

    def to_blocks(t):
        return jnp.moveaxis(t.reshape((B, nb, MLA_QBLOCK) + t.shape[2:]), 1, 0)

    def block(args):
        qn, qr = args
        s = (jnp.einsum('bqhd,bkhd->bhqk', qn, k_nope)
             + jnp.einsum('bqhr,bkr->bhqk', qr, k_rope))
        p = jax.nn.softmax(s.astype(jnp.float32) * scale, axis=-1).astype(v.dtype)
        return jnp.einsum('bhqk,bkhd->bqhd', p, v)

    o = lax.map(block, (to_blocks(q_nope), to_blocks(q_rope)))
    return jnp.moveaxis(o, 0, 1).reshape(B, S, MLA_HEADS * MLA_V)


def neighbourhood_attention(q, k, v, rpb):
    B, S, _ = q.shape
    rows = S // GRID_W
    kh = min(NA_WIN_H, rows)
    kw = min(NA_WIN_W, GRID_W)

    def grid(t):
        return t.reshape(B, rows, GRID_W, NA_HEADS, NA_HEAD_DIM)

    qg, kg, vg = grid(q), grid(k), grid(v)
    cols = jnp.arange(GRID_W)
    col_start = jnp.clip(cols - kw // 2, 0, GRID_W - kw)
    col_idx = col_start[:, None] + jnp.arange(kw)[None, :]
    col_off = col_idx - cols[:, None] + (NA_WIN_W - 1)
    scale = NA_HEAD_DIM ** -0.5

    def row_block(r):
        rs = jnp.clip(r - kh // 2, 0, rows - kh)
        k_rows = lax.dynamic_slice_in_dim(kg, rs, kh, axis=1)
        v_rows = lax.dynamic_slice_in_dim(vg, rs, kh, axis=1)
        k_win = k_rows[:, :, col_idx]
        v_win = v_rows[:, :, col_idx]
        q_row = lax.dynamic_index_in_dim(qg, r, axis=1, keepdims=False)
        row_off = rs + jnp.arange(kh) - r + (NA_WIN_H - 1)
        bias = jnp.transpose(rpb[:, row_off][:, :, col_off], (0, 2, 1, 3))
        s = (jnp.einsum('bchd,bicjhd->bhcij', q_row, k_win).astype(jnp.float32) * scale
             + bias.astype(jnp.float32)[None])
        p = jax.nn.softmax(s.reshape(B, NA_HEADS, GRID_W, kh * kw), axis=-1)
        p = p.reshape(s.shape).astype(v.dtype)
        return jnp.einsum('bhcij,bicjhd->bchd', p, v_win)

    o = lax.map(row_block, jnp.arange(rows))
    return jnp.moveaxis(o, 0, 1).reshape(B, S, NA_WIDTH)


def retention_scan(q, k, v, log_g, strict):
    B, H, S, dk = q.shape
    dv = v.shape[-1]
    C = RET_CHUNK
    n = S // C
    j = jnp.arange(C, dtype=jnp.float32)
    diff = j[:, None] - j[None, :]
    mask = (diff > 0) if strict else (diff >= 0)
    dmat = jnp.where(mask[None], jnp.exp(jnp.where(mask, diff, 0.0)[None] * log_g[:, None, None]),
                     0.0).astype(q.dtype)
    xi = jnp.exp((j + 1.0)[None] * log_g[:, None]).astype(q.dtype)
    zeta = jnp.exp((C - 1.0 - j)[None] * log_g[:, None]).astype(q.dtype)
    g_chunk = jnp.exp(C * log_g).astype(q.dtype)
    qc = q.reshape(B, H, n, C, dk)
    kc = k.reshape(B, H, n, C, dk)
    vc = v.reshape(B, H, n, C, dv)
    scores = jnp.einsum('bhncd,bhnmd->bhncm', qc, kc) * dmat[None, :, None]
    o_intra = jnp.einsum('bhncm,bhnme->bhnce', scores, vc)
    kv_chunk = jnp.einsum('bhncd,bhnce->bhnde', kc * zeta[None, :, None, :, None], vc)

    def step(state, kv_n):
        return g_chunk[None, :, None, None] * state + kv_n, state

    _, states = lax.scan(step, jnp.zeros((B, H, dk, dv), q.dtype), jnp.moveaxis(kv_chunk, 2, 0))
    states = jnp.moveaxis(states, 0, 2)
    o_cross = jnp.einsum('bhncd,bhnde->bhnce', qc, states) * xi[None, :, None, :, None]
    return (o_intra + o_cross).reshape(B, H, S, dv)


def bidirectional_retention(rq, rk, rv, rg, decay_fwd, decay_bwd):
    B, S, _ = rq.shape
    cos, sin = rope_tables(S, RET_QK_DIM)
    q = apply_rope(rq.reshape(B, S, RET_HEADS, RET_QK_DIM), cos, sin)
    k = apply_rope(rk.reshape(B, S, RET_HEADS, RET_QK_DIM), cos, sin) * (RET_QK_DIM ** -0.5)
    q = jnp.transpose(q, (0, 2, 1, 3))
    k = jnp.transpose(k, (0, 2, 1, 3))
    v = jnp.transpose(rv.reshape(B, S, RET_HEADS, RET_V_DIM), (0, 2, 1, 3))
    log_f = jax.nn.log_sigmoid(decay_fwd.astype(jnp.float32))
    log_b = jax.nn.log_sigmoid(decay_bwd.astype(jnp.float32))
    o_f = retention_scan(q, k, v, log_f, False)
    o_b = retention_scan(q[:, :, ::-1], k[:, :, ::-1], v[:, :, ::-1], log_b, True)[:, :, ::-1]
    o = jnp.transpose(o_f + o_b, (0, 2, 1, 3)).astype(jnp.float32)
    o = (o * lax.rsqrt(jnp.mean(o * o, axis=-1, keepdims=True) + EPS)).astype(rv.dtype)
    gate = jax.nn.silu(rg.reshape(B, S, RET_HEADS, RET_V_DIM))
    return (o * gate).reshape(B, S, RET_V_W)


def hyena_filters(length, w1, b1, w2, b2, w3, b3, w4, freq):
    t = jnp.arange(length, dtype=jnp.float32) / (length - 1)
    bands = (HY_EMB_DIM - 1) // 2
    w = 2.0 * math.pi * jnp.arange(length, dtype=jnp.float32) / length
    f = jnp.linspace(1e-4, bands - 1, bands, dtype=jnp.float32)
    fw = f[None, :] * w[:, None]
    z = jnp.concatenate([t[:, None], jnp.cos(fw), -jnp.sin(fw)], axis=-1).astype(w1.dtype)
    h = jnp.sin(freq * (z @ w1 + b1))
    h = jnp.sin(freq * (h @ w2 + b2))
    h = jnp.sin(freq * (h @ w3 + b3))
    h = h @ w4
    max_decay = math.log(HY_TARGET) / HY_FAST_DECAY
    min_decay = math.log(HY_TARGET) / HY_SLOW_DECAY
    deltas = jnp.linspace(min_decay, max_decay, HY_WIDTH, dtype=jnp.float32)
    window = jnp.exp(-t[:, None] * jnp.abs(deltas)[None, :]).astype(h.dtype)
    return h[:, :HY_WIDTH] * window, h[:, HY_WIDTH:] * window


def hyena_operator(u_proj, short_conv, w1, b1, w2, b2, w3, b3, w4, freq, hy_bias):
    B, L, _ = u_proj.shape
    z = dwconv3(u_proj, short_conv)
    x0, x1, v = jnp.split(z, 3, axis=-1)
    h_f, h_b = hyena_filters(L, w1, b1, w2, b2, w3, b3, w4, freq)
    k_circ = jnp.concatenate([h_f, jnp.zeros((1, HY_WIDTH), h_f.dtype), h_b[1:][::-1]],
                             axis=0).astype(jnp.float32)
    u = v * x1
    uf = jnp.fft.rfft(u.astype(jnp.float32), n=2 * L, axis=1)
    kf = jnp.fft.rfft(k_circ, n=2 * L, axis=0)
    y = jnp.fft.irfft(uf * kf[None], n=2 * L, axis=1)[:, :L].astype(u.dtype)
    y = y + u * hy_bias
    return y * x0


def even_mixer(h, w_in, q_norm, w_q_up, kv_norm, w_kv_up, rpb, w_out):
    cq, ckv, k_rope, nq, nk, nv = split_cols(
        h @ w_in, [MLA_Q_LORA, MLA_KV_LORA, MLA_ROPE, NA_WIDTH, NA_WIDTH, NA_WIDTH])
    a = mla_attention(cq, ckv, k_rope, q_norm, w_q_up, kv_norm, w_kv_up)
    b = neighbourhood_attention(nq, nk, nv, rpb)
    return jnp.concatenate([a, b], axis=-1) @ w_out


def odd_mixer(h, w_in, decay_fwd, decay_bwd, short_conv, w1, b1, w2, b2, w3, b3, w4, freq,
              hy_bias, w_out):
    rq, rk, rv, rg, hy = split_cols(h @ w_in, [RET_QK_W, RET_QK_W, RET_V_W, RET_V_W, 3 * HY_WIDTH])
    c = bidirectional_retention(rq, rk, rv, rg, decay_fwd, decay_bwd)
    d = hyena_operator(hy, short_conv, w1, b1, w2, b2, w3, b3, w4, freq, hy_bias)
    return jnp.concatenate([c, d], axis=-1) @ w_out


def conv_ffn(h, w_gate, w_up, conv_w, w_down):
    g = dwconv3(h @ w_gate, conv_w)
    return (jax.nn.gelu(g, approximate=True) * (h @ w_up)) @ w_down


def setup_inputs(seed: int = 0) -> dict:
    key = jax.random.key(seed)
    ks = iter(jax.random.split(key, 48))

    def nrm(shape, scale):
        return jax.random.normal(next(ks), shape, jnp.float32) * scale

    def gain(shape):
        return 1.0 + nrm(shape, 0.01)

    ret_decay_init = jnp.log(2.0 ** (5.0 + jnp.arange(RET_HEADS, dtype=jnp.float32)) - 1.0)
    return {
        'x': nrm((BATCH, SEQ, D_MODEL), 1.0),
        'mix_pre_norm': gain((DEPTH, D_MODEL)),
        'mix_post_norm': gain((DEPTH, D_MODEL)),
        'ffn_pre_norm': gain((DEPTH, D_MODEL)),
        'ffn_post_norm': gain((DEPTH, D_MODEL)),
        'ffn_w_gate': nrm((DEPTH, D_MODEL, D_FF), D_MODEL ** -0.5),
        'ffn_w_up': nrm((DEPTH, D_MODEL, D_FF), D_MODEL ** -0.5),
        'ffn_conv': nrm((DEPTH, SHORT_CONV, D_FF), SHORT_CONV ** -0.5),
        'ffn_w_down': nrm((DEPTH, D_FF, D_MODEL), D_FF ** -0.5),
        'a_w_in': nrm((N_EVEN, D_MODEL, A_IN), D_MODEL ** -0.5),
        'a_q_norm': gain((N_EVEN, MLA_Q_LORA)),
        'a_w_q_up': nrm((N_EVEN, MLA_Q_LORA, MLA_HEADS * (MLA_NOPE + MLA_ROPE)), MLA_Q_LORA ** -0.5),
        'a_kv_norm': gain((N_EVEN, MLA_KV_LORA)),
        'a_w_kv_up': nrm((N_EVEN, MLA_KV_LORA, MLA_HEADS * (MLA_NOPE + MLA_V)), MLA_KV_LORA ** -0.5),
        'a_rpb': nrm((N_EVEN, NA_HEADS, 2 * NA_WIN_H - 1, 2 * NA_WIN_W - 1), 0.1),
        'a_w_out': nrm((N_EVEN, A_OUT, D_MODEL), A_OUT ** -0.5),
        'c_w_in': nrm((N_ODD, D_MODEL, C_IN), D_MODEL ** -0.5),
        'c_decay_fwd': ret_decay_init[None] + nrm((N_ODD, RET_HEADS), 0.1),
        'c_decay_bwd': ret_decay_init[None] + nrm((N_ODD, RET_HEADS), 0.1),
        'c_short_conv': nrm((N_ODD, SHORT_CONV, 3 * HY_WIDTH), SHORT_CONV ** -0.5),
        'c_filt_w1': nrm((N_ODD, HY_EMB_DIM, HY_FILTER_HIDDEN), HY_EMB_DIM ** -0.5),
        'c_filt_b1': nrm((N_ODD, HY_FILTER_HIDDEN), 0.02),
        'c_filt_w2': nrm((N_ODD, HY_FILTER_HIDDEN, HY_FILTER_HIDDEN), HY_FILTER_HIDDEN ** -0.5),
        'c_filt_b2': nrm((N_ODD, HY_FILTER_HIDDEN), 0.02),
        'c_filt_w3': nrm((N_ODD, HY_FILTER_HIDDEN, HY_FILTER_HIDDEN), HY_FILTER_HIDDEN ** -0.5),
        'c_filt_b3': nrm((N_ODD, HY_FILTER_HIDDEN), 0.02),
        'c_filt_w4': nrm((N_ODD, HY_FILTER_HIDDEN, 2 * HY_WIDTH), 0.1 * HY_FILTER_HIDDEN ** -0.5),
        'c_filt_freq': gain((N_ODD, HY_FILTER_HIDDEN)),
        'c_hy_bias': nrm((N_ODD, HY_WIDTH), 1.0),
        'c_w_out': nrm((N_ODD, C_OUT, D_MODEL), C_OUT ** -0.5),
    }


def reference(x, mix_pre_norm, mix_post_norm, ffn_pre_norm, ffn_post_norm, ffn_w_gate, ffn_w_up,
              ffn_conv, ffn_w_down, a_w_in, a_q_norm, a_w_q_up, a_kv_norm, a_w_kv_up, a_rpb, a_w_out,
              c_w_in, c_decay_fwd, c_decay_bwd, c_short_conv, c_filt_w1, c_filt_b1, c_filt_w2,
              c_filt_b2, c_filt_w3, c_filt_b3, c_filt_w4, c_filt_freq, c_hy_bias, c_w_out):
    for layer in range(DEPTH):
        i = layer // 2
        h = rms_norm(x, mix_pre_norm[layer])
        if layer % 2 == 0:
            m = even_mixer(h, a_w_in[i], a_q_norm[i], a_w_q_up[i], a_kv_norm[i], a_w_kv_up[i],
                           a_rpb[i], a_w_out[i])
        else:
            m = odd_mixer(h, c_w_in[i], c_decay_fwd[i], c_decay_bwd[i], c_short_conv[i],
                          c_filt_w1[i], c_filt_b1[i], c_filt_w2[i], c_filt_b2[i], c_filt_w3[i],
                          c_filt_b3[i], c_filt_w4[i], c_filt_freq[i], c_hy_bias[i], c_w_out[i])
        x = x + rms_norm(m, mix_post_norm[layer])
        h = rms_norm(x, ffn_pre_norm[layer])
        f = conv_ffn(h, ffn_w_gate[layer], ffn_w_up[layer], ffn_conv[layer], ffn_w_down[layer])
        x = x + rms_norm(f, ffn_post_norm[layer])
    return x
```

```python
import functools
import math

import numpy as np
import jax
import jax.numpy as jnp
from jax import lax
from jax.experimental import pallas as pl
from jax.experimental.pallas import tpu as pltpu

F32 = jnp.float32
BF16 = jnp.bfloat16

D_MODEL = 1024
GRID_W = 64
MLA_HEADS = 8
MLA_Q_LORA = 256
MLA_KV_LORA = 128
MLA_NOPE = 64
MLA_ROPE = 32
MLA_V = 64
ROPE_THETA = 10000.0
NA_HEADS = 8
NA_HEAD_DIM = 64
NA_WIN_H = 8
NA_WIN_W = 16
NA_WIDTH = NA_HEADS * NA_HEAD_DIM
RET_HEADS = 4
RET_DIM = 128
RET_CHUNK = 128
RET_W = RET_HEADS * RET_DIM
HY_WIDTH = 512
HY_EMB_DIM = 33
HY_FILTER_HIDDEN = 64
HY_FAST_DECAY = 0.3
HY_SLOW_DECAY = 1.5
HY_TARGET = 1e-2
D_FF = 2816
EPS = 1e-6

LANES = 128
BF16_SUBLANES = 16
VMEM_LIMIT = 56 * 1024 * 1024

ROW_TILE = 512
MLA_Q_TILE = 256
MLA_K_CHUNK = 512
NA_ROWS = 4
NA_KEY_ROWS = 12
FFN_CHUNK = 256
HALO = BF16_SUBLANES

FFT_N1 = 128
FFT_N2 = 64
PITCH_T = 72
PITCH_A = 136
PITCH_B = 264


def _cparams(sem):
    return pltpu.CompilerParams(dimension_semantics=sem, vmem_limit_bytes=VMEM_LIMIT)


def _rms(x, g):
    return x * lax.rsqrt(jnp.mean(x * x, axis=-1, keepdims=True) + EPS) * g


def _dot(a, b):
    return jnp.dot(a, b, preferred_element_type=F32)


def _dot_nt(a, b):
    return lax.dot_general(a, b, (((1,), (1,)), ((), ())), preferred_element_type=F32)


def _const_spec(shape):
    nd = len(shape)
    return pl.BlockSpec(shape, lambda *_: (0,) * nd, pipeline_mode=pl.Buffered(1))


def _rope_tables(length, dim):
    inv = ROPE_THETA ** (-jnp.arange(0, dim, 2, dtype=F32) / dim)
    ang = jnp.arange(length, dtype=F32)[:, None] * inv[None, :]
    return jnp.cos(ang), jnp.sin(ang)


def _front0_kernel(x_ref, g_ref, wc_ref, wn_ref, qn_ref, kvn_ref, wq_ref, wqr_ref, wkv_ref,
                   wv_ref, vone_ref, tab_ref, q_ref, k_ref, v_ref, nq_ref, nk_ref, nv_ref):
    hn = _rms(x_ref[...], g_ref[...]).astype(BF16)
    c = _dot(hn, wc_ref[...])
    n = _dot(hn, wn_ref[...])
    nq_ref[...] = (n[:, :NA_WIDTH] * (NA_HEAD_DIM ** -0.5)).astype(BF16)
    nk_ref[...] = n[:, NA_WIDTH:2 * NA_WIDTH].astype(BF16)
    nv_ref[...] = n[:, 2 * NA_WIDTH:].astype(BF16)
    cqn = _rms(c[:, :MLA_Q_LORA], qn_ref[...]).astype(BF16)
    ckvn = _rms(c[:, MLA_Q_LORA:MLA_Q_LORA + MLA_KV_LORA], kvn_ref[...]).astype(BF16)
    kr = c[:, 384:512]
    krr = c[:, 512:640]
    cosq = tab_ref[:, 0:128]
    sinq = tab_ref[:, 128:256]
    cosk = tab_ref[:, 256:384]
    sink = tab_ref[:, 384:512]
    k_rope = kr * cosk + krr * sink
    q = _dot(cqn, wq_ref[...])
    qr = _dot(cqn, wqr_ref[...])
    kn = _dot(ckvn, wkv_ref[...])
    for h in range(MLA_HEADS):
        sl = slice(h * LANES, (h + 1) * LANES)
        q_ref[:, sl] = (q[:, sl] * cosq + qr[:, sl] * sinq).astype(BF16)
        k_ref[:, sl] = (kn[:, sl] + k_rope).astype(BF16)
    v_ref[...] = (_dot(ckvn, wv_ref[...]) + vone_ref[...]).astype(BF16)


def _front0(x2, g, a_w_in, a_q_norm, a_w_q_up, a_kv_norm, a_w_kv_up, seq):
    t_rows = x2.shape[0]
    tm = ROW_TILE
    n_seq = seq // tm
    w = a_w_in
    wkr = w[:, 384:416]
    zeros = lambda n: jnp.zeros((D_MODEL, n), F32)
    wkr_full = jnp.concatenate([zeros(64), wkr, zeros(32)], axis=1)
    wkr_rot = jnp.concatenate([zeros(64), -wkr[:, 16:], wkr[:, :16], zeros(32)], axis=1)
    wc = jnp.concatenate([w[:, :384], wkr_full, wkr_rot], axis=1).astype(BF16)
    wn = w[:, 416:].astype(BF16)
    wq3 = a_w_q_up.reshape(MLA_Q_LORA, MLA_HEADS, MLA_NOPE + MLA_ROPE)
    nope, rope = wq3[..., :MLA_NOPE], wq3[..., MLA_NOPE:]
    pad32 = jnp.zeros((MLA_Q_LORA, MLA_HEADS, 32), F32)
    pad64 = jnp.zeros((MLA_Q_LORA, MLA_HEADS, 64), F32)
    wq = jnp.concatenate([nope, rope, pad32], axis=-1).reshape(MLA_Q_LORA, -1).astype(BF16)
    wqr = jnp.concatenate([pad64, -rope[..., 16:], rope[..., :16], pad32],
                          axis=-1).reshape(MLA_Q_LORA, -1).astype(BF16)
    wkv3 = a_w_kv_up.reshape(MLA_KV_LORA, MLA_HEADS, MLA_NOPE + MLA_V)
    knope, vup = wkv3[..., :MLA_NOPE], wkv3[..., MLA_NOPE:]
    kpad = jnp.zeros((MLA_KV_LORA, MLA_HEADS, 64), F32)
    wkv = jnp.concatenate([knope, kpad], axis=-1).reshape(MLA_KV_LORA, -1).astype(BF16)
    vup4 = vup.reshape(MLA_KV_LORA, MLA_HEADS // 2, 2, MLA_V)
    vpad = jnp.zeros((MLA_KV_LORA, MLA_HEADS // 2, 64), F32)
    wv = jnp.concatenate([vup4[:, :, 0], vpad, vpad, vup4[:, :, 1]], axis=-1)
    wv = wv.reshape(MLA_KV_LORA, -1).astype(BF16)
    pair_one = np.zeros((256,), np.float32)
    pair_one[64] = 1.0
    pair_one[128] = 1.0
    vone = jnp.asarray(np.tile(pair_one, MLA_HEADS // 2)[None, :])
    cos, sin = _rope_tables(seq, MLA_ROPE)
    sc = (MLA_NOPE + MLA_ROPE) ** -0.5 * math.log2(math.e)
    z16 = jnp.zeros((seq, 32), F32)
    z64 = jnp.zeros((seq, 64), F32)
    tab = jnp.concatenate([
        jnp.full((seq, 64), sc, F32), cos * sc, cos * sc, z16,
        z64, sin * sc, sin * sc, z16,
        z64, cos, cos, z16,
        z64, sin, sin, z16], axis=1)
    out_shapes = [jax.ShapeDtypeStruct((t_rows, 1024), BF16)] * 3 + \
                 [jax.ShapeDtypeStruct((t_rows, NA_WIDTH), BF16)] * 3
    row = lambda n: pl.BlockSpec((tm, n), lambda i: (i, 0))
    return pl.pallas_call(
        _front0_kernel,
        grid=(t_rows // tm,),
        in_specs=[row(D_MODEL), _const_spec((1, D_MODEL)), _const_spec(wc.shape), _const_spec(wn.shape),
                  _const_spec((1, MLA_Q_LORA)), _const_spec((1, MLA_KV_LORA)), _const_spec(wq.shape),
                  _const_spec(wqr.shape), _const_spec(wkv.shape), _const_spec(wv.shape),
                  _const_spec((1, 1024)),
                  pl.BlockSpec((tm, 512), lambda i: (i % n_seq, 0))],
        out_specs=[row(1024), row(1024), row(1024), row(NA_WIDTH), row(NA_WIDTH), row(NA_WIDTH)],
        out_shape=out_shapes,
        compiler_params=_cparams(("parallel",)),
        name="front0",
    )(x2, g[None, :], wc, wn, a_q_norm[None, :], a_kv_norm[None, :], wq, wqr, wkv, wv, vone, tab)


def _mla_kernel(q_ref, k_ref, v_ref, o_ref, s_ref):
    tq = q_ref.shape[0]
    seq = k_ref.shape[0]
    n_chunks = seq // MLA_K_CHUNK
    accs = []
    for hh in range(2):
        hs = slice(hh * LANES, (hh + 1) * LANES)
        qh = q_ref[:, hs]
        m = jnp.full((tq, LANES), -jnp.inf, F32)
        for c in range(n_chunks):
            cs = slice(c * MLA_K_CHUNK, (c + 1) * MLA_K_CHUNK)
            s = _dot_nt(qh, k_ref[cs, hs])
            s_ref[:, cs] = s
            for j in range(MLA_K_CHUNK // LANES):
                m = jnp.maximum(m, s[:, j * LANES:(j + 1) * LANES])
        mrow = jnp.max(m, axis=-1, keepdims=True)
        acc = jnp.zeros((tq, LANES), F32)
        for c in range(n_chunks):
            cs = slice(c * MLA_K_CHUNK, (c + 1) * MLA_K_CHUNK)
            p = jnp.exp2(s_ref[:, cs] - mrow).astype(BF16)
            acc = acc + _dot(p, v_ref[cs, hs])
        accs.append(acc)
    l0 = accs[0][:, 64:65]
    l1 = accs[1][:, 0:1]
    lane = lax.broadcasted_iota(jnp.int32, (tq, LANES), 1)
    o_ref[...] = jnp.where(lane < 64, accs[0] / l0, accs[1] / l1).astype(BF16)


def _mla_attention(q, k, v2, batch, seq):
    tq = MLA_Q_TILE
    nq = seq // tq
    return pl.pallas_call(
        _mla_kernel,
        grid=(batch, MLA_HEADS // 2, nq),
        in_specs=[pl.BlockSpec((tq, 256), lambda b, p, i: (b * nq + i, p)),
                  pl.BlockSpec((seq, 256), lambda b, p, i: (b, p)),
                  pl.BlockSpec((seq, 256), lambda b, p, i: (b, p))],
        out_specs=pl.BlockSpec((tq, LANES), lambda b, p, i: (b * nq + i, p)),
        out_shape=jax.ShapeDtypeStruct((batch * seq, MLA_HEADS * MLA_V), BF16),
        scratch_shapes=[pltpu.VMEM((tq, seq), F32)],
        compiler_params=_cparams(("parallel", "parallel", "arbitrary")),
        name="mla_attention",
    )(q, k, v2)


def _na_bias_table(rpb, rows):
    kh, kw = NA_WIN_H, NA_WIN_W
    tables = []
    for r0, ws in ((0, 0), (2 * NA_ROWS, 2 * NA_ROWS - kh // 2), (rows - NA_ROWS, rows - NA_KEY_ROWS)):
        r = r0 + np.arange(NA_ROWS)[:, None, None, None]
        c = np.arange(GRID_W)[None, :, None, None]
        kr = ws + np.arange(NA_KEY_ROWS)[None, None, :, None]
        kc = np.arange(GRID_W)[None, None, None, :]
        rs = np.clip(r - kh // 2, 0, rows - kh)
        cst = np.clip(c - kw // 2, 0, GRID_W - kw)
        valid = (kr >= rs) & (kr < rs + kh) & (kc >= cst) & (kc < cst + kw)
        row_off = np.clip(kr - r + (kh - 1), 0, 2 * kh - 2)
        col_off = np.clip(kc - c + (kw - 1), 0, 2 * kw - 2)
        shape = (NA_ROWS, GRID_W, NA_KEY_ROWS, GRID_W)
        row_off = np.broadcast_to(row_off, shape)
        col_off = np.broadcast_to(col_off, shape)
        valid = np.broadcast_to(valid, shape)
        bias = rpb[:, row_off, col_off]
        bias = jnp.where(valid[None], bias, -1e30)
        tables.append(bias.reshape(NA_HEADS, NA_ROWS * GRID_W, NA_KEY_ROWS * GRID_W))
    return jnp.stack(tables).astype(F32)


def _na_kernel(q_ref, k_ref, v_ref, b_ref, o_ref):
    g = pl.program_id(1)
    rows = k_ref.shape[0] // GRID_W
    ws = jnp.clip(g * NA_ROWS - NA_WIN_H // 2, 0, rows - NA_KEY_ROWS)
    start = pl.multiple_of(ws * GRID_W, GRID_W)
    nkeys = NA_KEY_ROWS * GRID_W
    nqry = NA_ROWS * GRID_W
    lane = lax.broadcasted_iota(jnp.int32, (1, LANES), 1)
    for p in range(NA_HEADS // 2):
        ps = slice(p * LANES, (p + 1) * LANES)
        q2 = q_ref[:, ps]
        k2 = k_ref[pl.ds(start, nkeys), ps]
        v2 = v_ref[pl.ds(start, nkeys), ps]
        o = jnp.zeros((nqry, LANES), F32)
        for hh in range(2):
            mask = (lane < 64) if hh == 0 else (lane >= 64)
            qm = jnp.where(mask, q2, jnp.zeros_like(q2))
            vm = jnp.where(mask, v2, jnp.zeros_like(v2))
            s = _dot_nt(qm, k2) + b_ref[0, 2 * p + hh]
            m = jnp.max(s, axis=-1, keepdims=True)
            e = jnp.exp(s - m)
            l = jnp.sum(e, axis=-1, keepdims=True)
            o = o + _dot(e.astype(BF16), vm) / l
        o_ref[:, ps] = o.astype(BF16)


def _na_attention(nq, nk, nv, rpb, batch, seq):
    rows = seq // GRID_W
    groups = rows // NA_ROWS
    nqry = NA_ROWS * GRID_W
    nkeys = NA_KEY_ROWS * GRID_W
    bias = _na_bias_table(rpb, rows)
    case = lambda g: (g > 0).astype(jnp.int32) + (g == groups - 1).astype(jnp.int32)
    return pl.pallas_call(
        _na_kernel,
        grid=(batch, groups),
        in_specs=[pl.BlockSpec((nqry, NA_WIDTH), lambda b, g: (b * groups + g, 0)),
                  pl.BlockSpec((seq, NA_WIDTH), lambda b, g: (b, 0)),
                  pl.BlockSpec((seq, NA_WIDTH), lambda b, g: (b, 0)),
                  pl.BlockSpec((1, NA_HEADS, nqry, nkeys), lambda b, g: (case(g), 0, 0, 0))],
        out_specs=pl.BlockSpec((nqry, NA_WIDTH), lambda b, g: (b * groups + g, 0)),
        out_shape=jax.ShapeDtypeStruct((batch * seq, NA_WIDTH), BF16),
        compiler_params=_cparams(("parallel", "arbitrary")),
        name="na_attention",
    )(nq, nk, nv, bias)


def _outproj_kernel(a_ref, b_ref, wa_ref, wb_ref, x_ref, g_ref, o_ref):
    m = _dot(a_ref[...].astype(BF16), wa_ref[...]) + _dot(b_ref[...].astype(BF16), wb_ref[...])
    o_ref[...] = x_ref[...] + _rms(m, g_ref[...])


def _outproj(a, b, w_out, x2, g):
    t_rows = x2.shape[0]
    tm = ROW_TILE
    half = w_out.shape[0] // 2
    wa = w_out[:half].astype(BF16)
    wb = w_out[half:].astype(BF16)
    row = lambda n: pl.BlockSpec((tm, n), lambda i: (i, 0))
    return pl.pallas_call(
        _outproj_kernel,
        grid=(t_rows // tm,),
        in_specs=[row(half), row(half), _const_spec(wa.shape), _const_spec(wb.shape), row(D_MODEL),
                  _const_spec((1, D_MODEL))],
        out_specs=row(D_MODEL),
        out_shape=jax.ShapeDtypeStruct((t_rows, D_MODEL), F32),
        compiler_params=_cparams(("parallel",)),
        name="outproj",
    )(a, b, wa, wb, x2, g[None, :])


def _gelu_tanh(x):
    return 0.5 * x * (1.0 + jnp.tanh(math.sqrt(2.0 / math.pi) * (x + 0.044715 * (x * x * x))))


def _ffn_kernel(x_ref, xp_ref, xn_ref, gpre_ref, wg_ref, wu_ref, cw_ref, wd_ref, gpost_ref, o_ref,
                a_ref, *, n_seq):
    i = pl.program_id(0)
    tm = x_ref.shape[0]
    x = x_ref[...]
    gpre = gpre_ref[...]
    hn = _rms(x, gpre)
    keep_prev = jnp.where(i % n_seq == 0, 0.0, 1.0)
    keep_next = jnp.where(i % n_seq == n_seq - 1, 0.0, 1.0)
    hp = _rms(xp_ref[...], gpre) * keep_prev
    hx = _rms(xn_ref[...], gpre) * keep_next
    h_ext = jnp.concatenate([hp, hn, hx], axis=0).astype(BF16)
    hb = hn.astype(BF16)
    ext = tm + 2 * HALO
    for j in range(D_FF // FFN_CHUNK):
        cs = slice(j * FFN_CHUNK, (j + 1) * FFN_CHUNK)
        gate = _dot(h_ext, wg_ref[:, cs])
        up = _dot(hb, wu_ref[:, cs])
        g_prev = pltpu.roll(gate, 1, 0)[HALO:HALO + tm]
        g_next = pltpu.roll(gate, ext - 1, 0)[HALO:HALO + tm]
        conv = g_prev * cw_ref[0:1, cs] + gate[HALO:HALO + tm] * cw_ref[1:2, cs] + g_next * cw_ref[2:3, cs]
        a_ref[:, cs] = (_gelu_tanh(conv) * up).astype(BF16)
    f = _dot(a_ref[...], wd_ref[...])
    o_ref[...] = x + _rms(f, gpost_ref[...])


def _ffn(x2, gpre, w_gate, w_up, conv_w, w_down, gpost, seq):
    t_rows = x2.shape[0]
    tm = ROW_TILE
    n_seq = seq // tm
    hb = tm // HALO
    n_halo = t_rows // HALO
    row = pl.BlockSpec((tm, D_MODEL), lambda i: (i, 0))
    prev = pl.BlockSpec((HALO, D_MODEL), lambda i: (jnp.maximum(i * hb - 1, 0), 0))
    nxt = pl.BlockSpec((HALO, D_MODEL), lambda i: (jnp.minimum((i + 1) * hb, n_halo - 1), 0))
    return pl.pallas_call(
        functools.partial(_ffn_kernel, n_seq=n_seq),
        grid=(t_rows // tm,),
        in_specs=[row, prev, nxt, _const_spec((1, D_MODEL)), _const_spec((D_MODEL, D_FF)),
                  _const_spec((D_MODEL, D_FF)), _const_spec((3, D_FF)), _const_spec((D_FF, D_MODEL)),
                  _const_spec((1, D_MODEL))],
        out_specs=row,
        out_shape=jax.ShapeDtypeStruct((t_rows, D_MODEL), F32),
        scratch_shapes=[pltpu.VMEM((tm, D_FF), BF16)],
        compiler_params=_cparams(("parallel",)),
        name="ffn",
    )(x2, x2, x2, gpre[None, :], w_gate.astype(BF16), w_up.astype(BF16), conv_w,
      w_down.astype(BF16), gpost[None, :])


def _front1_kernel(x_ref, g_ref, w_ref, tab_ref, rq_ref, rk_ref, rv_ref, rg_ref, hy_ref):
    hn = _rms(x_ref[...], g_ref[...]).astype(BF16)
    r = _dot(hn, w_ref[:, :4 * RET_W])
    cosr = tab_ref[:, 0:128]
    sinr = tab_ref[:, 128:256]
    kscale = RET_DIM ** -0.5
    for h in range(RET_HEADS):
        qs = slice(h * RET_DIM, (h + 1) * RET_DIM)
        ks = slice(RET_W + h * RET_DIM, RET_W + (h + 1) * RET_DIM)
        qh = r[:, qs]
        kh = r[:, ks]
        rq_ref[:, qs] = (qh * cosr + pltpu.roll(qh, RET_DIM // 2, 1) * sinr).astype(BF16)
        rk_ref[:, qs] = ((kh * cosr + pltpu.roll(kh, RET_DIM // 2, 1) * sinr) * kscale).astype(BF16)
    rv_ref[...] = r[:, 2 * RET_W:3 * RET_W].astype(BF16)
    rg_ref[...] = r[:, 3 * RET_W:]
    hy_ref[...] = _dot(hn, w_ref[:, 4 * RET_W:]).astype(BF16)


def _front1(x2, g, c_w_in, seq):
    t_rows = x2.shape[0]
    tm = ROW_TILE
    n_seq = seq // tm
    cos, sin = _rope_tables(seq, RET_DIM)
    tab = jnp.concatenate([cos, cos, -sin, sin], axis=1)
    w = c_w_in.astype(BF16)
    row = lambda n: pl.BlockSpec((tm, n), lambda i: (i, 0))
    out_shapes = [jax.ShapeDtypeStruct((t_rows, RET_W), BF16)] * 3 + \
                 [jax.ShapeDtypeStruct((t_rows, RET_W), F32),
                  jax.ShapeDtypeStruct((t_rows, 3 * HY_WIDTH), BF16)]
    return pl.pallas_call(
        _front1_kernel,
        grid=(t_rows // tm,),
        in_specs=[row(D_MODEL), _const_spec((1, D_MODEL)), _const_spec(w.shape),
                  pl.BlockSpec((tm, 256), lambda i: (i % n_seq, 0))],
        out_specs=[row(RET_W), row(RET_W), row(RET_W), row(RET_W), row(3 * HY_WIDTH)],
        out_shape=out_shapes,
        compiler_params=_cparams(("parallel",)),
        name="front1",
    )(x2, g[None, :], w, tab)


def _log_sigmoid(x):
    return jnp.minimum(x, 0.0) - jnp.log1p(jnp.exp(-jnp.abs(x)))


def _ret_kernel(q_ref, k_ref, v_ref, g_ref, dec_ref, o_ref, intra_ref, kvf_ref, kvb_ref,
                sf_ref, sb_ref):
    seq = q_ref.shape[0]
    c = RET_CHUNK
    n_chunks = seq // c
    lf = _log_sigmoid(dec_ref[0, 0:1, :])
    lb = _log_sigmoid(dec_ref[0, 1:2, :])
    t = lax.broadcasted_iota(jnp.int32, (c, c), 0).astype(F32)
    s_idx = lax.broadcasted_iota(jnp.int32, (c, c), 1).astype(F32)
    diff = t - s_idx
    dmat = jnp.where(diff >= 0, jnp.exp(jnp.maximum(diff, 0.0) * lf),
                     jnp.exp(jnp.maximum(-diff, 0.0) * lb))
    xi_f = jnp.exp((t + 1.0) * lf)
    xi_b = jnp.exp((c - t) * lb)
    zeta_f = jnp.exp((c - 1.0 - t) * lf)
    zeta_b = jnp.exp(t * lb)
    g_f = jnp.exp(c * lf)
    g_b = jnp.exp(c * lb)

    def chunk_a(n, carry):
        rs = pl.ds(pl.multiple_of(n * c, c), c)
        qn = q_ref[rs, :]
        kn = k_ref[rs, :]
        vn = v_ref[rs, :]
        sc = (_dot_nt(qn, kn) * dmat).astype(BF16)
        intra_ref[rs, :] = _dot(sc, vn)
        knf = kn.astype(F32)
        kvf_ref[rs, :] = _dot((knf * zeta_f).T.astype(BF16), vn)
        kvb_ref[rs, :] = _dot((knf * zeta_b).T.astype(BF16), vn)
        return carry

    lax.fori_loop(0, n_chunks, chunk_a, 0)

    def scan_f(n, st):
        rs = pl.ds(pl.multiple_of(n * c, c), c)
        sf_ref[rs, :] = st.astype(BF16)
        return g_f * st + kvf_ref[rs, :]

    lax.fori_loop(0, n_chunks, scan_f, jnp.zeros((c, c), F32))

    def scan_b(m, st):
        n = n_chunks - 1 - m
        rs = pl.ds(pl.multiple_of(n * c, c), c)
        sb_ref[rs, :] = st.astype(BF16)
        return g_b * st + kvb_ref[rs, :]

    lax.fori_loop(0, n_chunks, scan_b, jnp.zeros((c, c), F32))

    def chunk_c(n, carry):
        rs = pl.ds(pl.multiple_of(n * c, c), c)
        qn = q_ref[rs, :]
        o = intra_ref[rs, :] + _dot(qn, sf_ref[rs, :]) * xi_f + _dot(qn, sb_ref[rs, :]) * xi_b
        o = o * lax.rsqrt(jnp.mean(o * o, axis=-1, keepdims=True) + EPS)
        gate = g_ref[rs, :]
        o_ref[rs, :] = (o * (gate * jax.nn.sigmoid(gate))).astype(BF16)
        return carry

    lax.fori_loop(0, n_chunks, chunk_c, 0)


def _retention(rq, rk, rv, rg, decay_fwd, decay_bwd, batch, seq):
    dec = jnp.stack([decay_fwd, decay_bwd], axis=1)
    dec = jnp.broadcast_to(dec[:, :, None], (RET_HEADS, 2, LANES)).astype(F32)
    blk = pl.BlockSpec((seq, RET_DIM), lambda b, h: (b, h))
    return pl.pallas_call(
        _ret_kernel,
        grid=(batch, RET_HEADS),
        in_specs=[blk, blk, blk, blk, pl.BlockSpec((1, 2, LANES), lambda b, h: (h, 0, 0))],
        out_specs=blk,
        out_shape=jax.ShapeDtypeStruct((batch * seq, RET_W), BF16),
        scratch_shapes=[pltpu.VMEM((seq, RET_DIM), F32), pltpu.VMEM((seq, RET_DIM), F32),
                        pltpu.VMEM((seq, RET_DIM), F32), pltpu.VMEM((seq, RET_DIM), BF16),
                        pltpu.VMEM((seq, RET_DIM), BF16)],
        compiler_params=_cparams(("parallel", "parallel")),
        name="retention",
    )(rq, rk, rv, rg, dec)


def _fft_tables():
    n = FFT_N1 * FFT_N2
    n2 = np.arange(FFT_N2)[:, None, None]
    k1 = np.arange(FFT_N1)[None, :, None]
    n1 = np.arange(FFT_N1 // 2)[None, None, :]
    ang = 2.0 * np.pi * ((k1 * (FFT_N2 * n1 + n2)) % n) / n
    f1 = np.concatenate([np.cos(ang), -np.sin(ang)], axis=1)
    angt = ang.transpose(0, 2, 1)
    g3 = np.concatenate([np.cos(angt), -np.sin(angt)], axis=2)
    a = np.arange(FFT_N2)
    ang2 = 2.0 * np.pi * ((a[:, None] * a[None, :]) % FFT_N2) / FFT_N2
    cc, ss = np.cos(ang2), np.sin(ang2)
    g2 = np.block([[cc, ss], [-ss, cc]])
    g2i = np.block([[cc, -ss], [ss, cc]])
    return f1, g2, g2i, g3


def _split_bf16(a):
    hi = jnp.asarray(a, F32).astype(BF16)
    lo = (jnp.asarray(a, F32) - hi.astype(F32)).astype(BF16)
    return hi, lo


def _dot3(a_hi, a_lo, x):
    x_hi = x.astype(BF16)
    x_lo = (x - x_hi.astype(F32)).astype(BF16)
    return _dot(a_hi, x_hi) + _dot(a_hi, x_lo) + _dot(a_lo, x_hi)


def _hyfilt_kernel(z_ref, w1_ref, b1_ref, w2_ref, b2_ref, w3_ref, b3_ref, fr_ref, w4f_ref, w4b_ref,
                   dl_ref, f1h_ref, f1l_ref, g2h_ref, g2l_ref, kf_ref, t_ref, a_ref):
    seq = z_ref.shape[0]
    hp = lax.Precision.HIGHEST
    blk = FFT_N2
    inv_n = 1.0 / (FFT_N1 * FFT_N2)
    fr = fr_ref[...]
    dl = jnp.abs(dl_ref[...])

    def dense(h, w_ref, b_ref):
        return jnp.sin(fr * (jnp.dot(h, w_ref[...], precision=hp, preferred_element_type=F32) + b_ref[...]))

    for w4_ref, backward in ((w4f_ref, False), (w4b_ref, True)):

        def fill(n1, carry):
            row0 = pl.multiple_of(n1 * blk, blk)
            h = dense(dense(dense(z_ref[pl.ds(row0, blk), :], w1_ref, b1_ref), w2_ref, b2_ref),
                      w3_ref, b3_ref)
            t_idx = row0 + lax.broadcasted_iota(jnp.int32, (blk, LANES), 0)
            window = jnp.exp(-(t_idx.astype(F32) / (seq - 1)) * dl)
            filt = jnp.dot(h, w4_ref[...], precision=hp, preferred_element_type=F32) * window
            if backward:
                filt = jnp.where(t_idx == 0, 0.0, filt)
            t_ref[pl.ds(pl.multiple_of(n1 * PITCH_T, 8), blk), :] = filt
            return carry

        lax.fori_loop(0, FFT_N1 // 2, fill, 0)

        def stage1(n2, carry):
            xs = t_ref[pl.ds(n2, FFT_N1 // 2, stride=PITCH_T), :]
            r = _dot3(f1h_ref[n2], f1l_ref[n2], xs)
            a_ref[pl.ds(n2, FFT_N1, stride=PITCH_A), :] = r[:FFT_N1]
            a_ref[pl.ds(FFT_N2 + n2, FFT_N1, stride=PITCH_A), :] = r[FFT_N1:]
            return carry

        lax.fori_loop(0, FFT_N2, stage1, 0)

        def stage2(k1, carry):
            rs = pl.ds(pl.multiple_of(k1 * PITCH_A, 8), 2 * FFT_N2)
            y = _dot3(g2h_ref[...], g2l_ref[...], a_ref[rs, :]) * inv_n
            if backward:
                kf_ref[k1] = kf_ref[k1] + jnp.concatenate([y[:FFT_N2], -y[FFT_N2:]], axis=0)
            else:
                kf_ref[k1] = y
            return carry

        lax.fori_loop(0, FFT_N1, stage2, 0)


def _hyena_filter_spectrum(w1, b1, w2, b2, w3, b3, w4, freq, seq, tables):
    f1, g2, _, _ = tables
    t = jnp.arange(seq, dtype=F32) / (seq - 1)
    bands = (HY_EMB_DIM - 1) // 2
    w = 2.0 * math.pi * jnp.arange(seq, dtype=F32) / seq
    f = jnp.linspace(1e-4, bands - 1, bands, dtype=F32)
    fw = f[None, :] * w[:, None]
    z = jnp.concatenate([t[:, None], jnp.cos(fw), -jnp.sin(fw)], axis=-1)
    z = jnp.pad(z, ((0, 0), (0, LANES - HY_EMB_DIM)))
    w1p = jnp.pad(w1, ((0, LANES - HY_EMB_DIM), (0, 0)))
    max_decay = math.log(HY_TARGET) / HY_FAST_DECAY
    min_decay = math.log(HY_TARGET) / HY_SLOW_DECAY
    deltas = jnp.linspace(min_decay, max_decay, HY_WIDTH, dtype=F32)[None, :]
    f1h, f1l = _split_bf16(f1)
    g2h, g2l = _split_bf16(g2)
    n_ct = HY_WIDTH // LANES
    hid = HY_FILTER_HIDDEN
    rows_t = (FFT_N1 // 2) * PITCH_T
    rows_a = FFT_N1 * PITCH_A
    return pl.pallas_call(
        _hyfilt_kernel,
        grid=(n_ct,),
        in_specs=[_const_spec((seq, LANES)), _const_spec((LANES, hid)), _const_spec((1, hid)),
                  _const_spec((hid, hid)), _const_spec((1, hid)), _const_spec((hid, hid)),
                  _const_spec((1, hid)), _const_spec((1, hid)),
                  pl.BlockSpec((hid, LANES), lambda c: (0, c)),
                  pl.BlockSpec((hid, LANES), lambda c: (0, n_ct + c)),
                  pl.BlockSpec((1, LANES), lambda c: (0, c)),
                  _const_spec(f1h.shape), _const_spec(f1l.shape), _const_spec(g2h.shape),
                  _const_spec(g2l.shape)],
        out_specs=pl.BlockSpec((FFT_N1, 2 * FFT_N2, LANES), lambda c: (0, 0, c)),
        out_shape=jax.ShapeDtypeStruct((FFT_N1, 2 * FFT_N2, HY_WIDTH), F32),
        scratch_shapes=[pltpu.VMEM((rows_t, LANES), F32), pltpu.VMEM((rows_a, LANES), F32)],
        compiler_params=_cparams(("arbitrary",)),
        name="hyena_filter",
    )(z, w1p, b1[None, :], w2, b2[None, :], w3, b3[None, :], freq[None, :], w4, w4, deltas,
      f1h, f1l, g2h, g2l)


def _hyconv_kernel(x0_ref, x1_ref, v_ref, c0_ref, c1_ref, cv_ref, bias_ref, kf_ref, f1_ref, g2_ref,
                   g2i_ref, g3_ref, o_ref, u_ref, z0_ref, a_ref, b_ref):
    seq = x0_ref.shape[0]
    t_idx = lax.broadcasted_iota(jnp.int32, (seq, LANES), 0)

    def dwconv(ref, w_ref):
        x = ref[...].astype(F32)
        prev = jnp.where(t_idx == 0, 0.0, pltpu.roll(x, 1, 0))
        nxt = jnp.where(t_idx == seq - 1, 0.0, pltpu.roll(x, seq - 1, 0))
        return prev * w_ref[0:1, :] + x * w_ref[1:2, :] + nxt * w_ref[2:3, :]

    z0 = dwconv(x0_ref, c0_ref)
    u = dwconv(v_ref, cv_ref) * dwconv(x1_ref, c1_ref)
    blk = FFT_N2
    for n1 in range(FFT_N1 // 2):
        u_ref[n1 * PITCH_T:n1 * PITCH_T + blk, :] = u[n1 * blk:(n1 + 1) * blk]
        z0_ref[n1 * PITCH_T:n1 * PITCH_T + blk, :] = z0[n1 * blk:(n1 + 1) * blk]

    def stage1(n2, carry):
        xs = u_ref[pl.ds(n2, FFT_N1 // 2, stride=PITCH_T), :].astype(BF16)
        r = _dot(f1_ref[n2], xs)
        a_ref[pl.ds(n2, FFT_N1, stride=PITCH_A), :] = r[:FFT_N1]
        a_ref[pl.ds(FFT_N2 + n2, FFT_N1, stride=PITCH_A), :] = r[FFT_N1:]
        return carry

    lax.fori_loop(0, FFT_N2, stage1, 0)

    def stage2(k1, carry):
        rs = pl.ds(pl.multiple_of(k1 * PITCH_A, 8), 2 * FFT_N2)
        y = _dot(g2_ref[...], a_ref[rs, :].astype(BF16))
        kf = kf_ref[k1]
        yr, yi = y[:FFT_N2], y[FFT_N2:]
        kr, ki = kf[:FFT_N2], kf[FFT_N2:]
        z = jnp.concatenate([yr * kr - yi * ki, yr * ki + yi * kr], axis=0).astype(BF16)
        bv = _dot(g2i_ref[...], z)
        b_ref[pl.ds(k1, FFT_N2, stride=PITCH_B), :] = bv[:FFT_N2]
        b_ref[pl.ds(FFT_N1 + k1, FFT_N2, stride=PITCH_B), :] = bv[FFT_N2:]
        return carry

    lax.fori_loop(0, FFT_N1, stage2, 0)
    bias = bias_ref[...]

    def stage3(n2, carry):
        rs = pl.ds(pl.multiple_of(n2 * PITCH_B, 8), 2 * FFT_N1)
        y = _dot(g3_ref[n2], b_ref[rs, :].astype(BF16))
        us = u_ref[pl.ds(n2, FFT_N1 // 2, stride=PITCH_T), :]
        zs = z0_ref[pl.ds(n2, FFT_N1 // 2, stride=PITCH_T), :]
        z0_ref[pl.ds(n2, FFT_N1 // 2, stride=PITCH_T), :] = (y + us * bias) * zs
        return carry

    lax.fori_loop(0, FFT_N2, stage3, 0)
    for n1 in range(FFT_N1 // 2):
        o_ref[n1 * blk:(n1 + 1) * blk, :] = z0_ref[n1 * PITCH_T:n1 * PITCH_T + blk, :].astype(BF16)


def _hyena_conv(hy, short_conv, hy_bias, kf, batch, seq, tables):
    f1, g2, g2i, g3 = tables
    n_ct = HY_WIDTH // LANES
    rows_t = (FFT_N1 // 2) * PITCH_T
    col = lambda off: pl.BlockSpec((seq, LANES), lambda c, b: (b, off + c))
    cw = lambda off: pl.BlockSpec((3, LANES), lambda c, b: (0, off + c))
    return pl.pallas_call(
        _hyconv_kernel,
        grid=(n_ct, batch),
        in_specs=[col(0), col(n_ct), col(2 * n_ct), cw(0), cw(n_ct), cw(2 * n_ct),
                  pl.BlockSpec((1, LANES), lambda c, b: (0, c)),
                  pl.BlockSpec((FFT_N1, 2 * FFT_N2, LANES), lambda c, b: (0, 0, c),
                               pipeline_mode=pl.Buffered(1)),
                  _const_spec(f1.shape), _const_spec(g2.shape), _const_spec(g2i.shape),
                  _const_spec(g3.shape)],
        out_specs=pl.BlockSpec((seq, LANES), lambda c, b: (b, c)),
        out_shape=jax.ShapeDtypeStruct((batch * seq, HY_WIDTH), BF16),
        scratch_shapes=[pltpu.VMEM((rows_t, LANES), F32), pltpu.VMEM((rows_t, LANES), F32),
                        pltpu.VMEM((FFT_N1 * PITCH_A, LANES), F32),
                        pltpu.VMEM((FFT_N2 * PITCH_B, LANES), F32)],
        compiler_params=_cparams(("arbitrary", "arbitrary")),
        name="hyena_conv",
    )(hy, hy, hy, short_conv, short_conv, short_conv, hy_bias[None, :], kf,
      jnp.asarray(f1, F32).astype(BF16), jnp.asarray(g2, F32).astype(BF16),
      jnp.asarray(g2i, F32).astype(BF16), jnp.asarray(g3, F32).astype(BF16))


def kernel(x, mix_pre_norm, mix_post_norm, ffn_pre_norm, ffn_post_norm, ffn_w_gate, ffn_w_up, ffn_conv, ffn_w_down, a_w_in, a_q_norm, a_w_q_up, a_kv_norm, a_w_kv_up, a_rpb, a_w_out, c_w_in, c_decay_fwd, c_decay_bwd, c_short_conv, c_filt_w1, c_filt_b1, c_filt_w2, c_filt_b2, c_filt_w3, c_filt_b3, c_filt_w4, c_filt_freq, c_hy_bias, c_w_out):
    batch, seq, d = x.shape
    assert d == D_MODEL and seq * 2 == FFT_N1 * FFT_N2 and seq % ROW_TILE == 0
    x2 = x.reshape(batch * seq, d)

    q, k, v2, nq, nk, nv = _front0(x2, mix_pre_norm[0], a_w_in[0], a_q_norm[0], a_w_q_up[0],
                                   a_kv_norm[0], a_w_kv_up[0], seq)
    a = _mla_attention(q, k, v2, batch, seq)
    b = _na_attention(nq, nk, nv, a_rpb[0], batch, seq)
    x2 = _outproj(a, b, a_w_out[0], x2, mix_post_norm[0])
    x2 = _ffn(x2, ffn_pre_norm[0], ffn_w_gate[0], ffn_w_up[0], ffn_conv[0], ffn_w_down[0],
              ffn_post_norm[0], seq)

    rq, rk, rv, rg, hy = _front1(x2, mix_pre_norm[1], c_w_in[0], seq)
    c = _retention(rq, rk, rv, rg, c_decay_fwd[0], c_decay_bwd[0], batch, seq)
    tables = _fft_tables()
    kf = _hyena_filter_spectrum(c_filt_w1[0], c_filt_b1[0], c_filt_w2[0], c_filt_b2[0], c_filt_w3[0],
                                c_filt_b3[0], c_filt_w4[0], c_filt_freq[0], seq, tables)
    dd = _hyena_conv(hy, c_short_conv[0], c_hy_bias[0], kf, batch, seq, tables)
    x2 = _outproj(c, dd, c_w_out[0], x2, mix_post_norm[1])
    x2 = _ffn(x2, ffn_pre_norm[1], ffn_w_gate[1], ffn_w_up[1], ffn_conv[1], ffn_w_down[1],
              ffn_post_norm[1], seq)
    return x2.reshape(batch, seq, d)
```

```python
import functools
import math

import numpy as np
import jax
import jax.numpy as jnp
from jax import lax
from jax.experimental import pallas as pl
from jax.experimental.pallas import tpu as pltpu

F32 = jnp.float32
BF16 = jnp.bfloat16

D_MODEL = 1024
GRID_W = 64
MLA_HEADS = 8
MLA_Q_LORA = 256
MLA_KV_LORA = 128
MLA_NOPE = 64
MLA_ROPE = 32
MLA_V = 64
ROPE_THETA = 10000.0
NA_HEADS = 8
NA_HEAD_DIM = 64
NA_WIN_H = 8
NA_WIN_W = 16
NA_WIDTH = NA_HEADS * NA_HEAD_DIM
RET_HEADS = 4
RET_DIM = 128
RET_CHUNK = 128
RET_W = RET_HEADS * RET_DIM
HY_WIDTH = 512
HY_EMB_DIM = 33
HY_FILTER_HIDDEN = 64
HY_FAST_DECAY = 0.3
HY_SLOW_DECAY = 1.5
HY_TARGET = 1e-2
D_FF = 2816
EPS = 1e-6

LANES = 128
BF16_SUBLANES = 16
VMEM_LIMIT = 56 * 1024 * 1024
HYCONV_VMEM_LIMIT = 60 * 1024 * 1024

ROW_TILE = 512
MLA_Q_TILE = 256
MLA_K_CHUNK = 512
NA_ROWS = 4
NA_KEY_ROWS = 12
FFN_CHUNK = 256
RET_UNROLL = 4
FFT_UNROLL = 4
FFT_GROUP = 2
HY_MLP_ROWS = 256
HALO = BF16_SUBLANES

FFT_N1 = 128
FFT_N2 = 64
PITCH_T = 72
PITCH_A = 136
PITCH_B = 264


def _cparams(sem, vmem_limit=VMEM_LIMIT):
    return pltpu.CompilerParams(dimension_semantics=sem, vmem_limit_bytes=vmem_limit)


def _rms(x, g):
    return x * lax.rsqrt(jnp.mean(x * x, axis=-1, keepdims=True) + EPS) * g


def _dot(a, b):
    return jnp.dot(a, b, preferred_element_type=F32)


def _dot_nt(a, b):
    return lax.dot_general(a, b, (((1,), (1,)), ((), ())), preferred_element_type=F32)


def _aligned(x, m):
    return x if isinstance(x, int) else pl.multiple_of(x, m)


def _const_spec(shape):
    nd = len(shape)
    return pl.BlockSpec(shape, lambda *_: (0,) * nd, pipeline_mode=pl.Buffered(1))


def _rope_tables(length, dim):
    inv = ROPE_THETA ** (-jnp.arange(0, dim, 2, dtype=F32) / dim)
    ang = jnp.arange(length, dtype=F32)[:, None] * inv[None, :]
    return jnp.cos(ang), jnp.sin(ang)


def _front0_kernel(x_ref, g_ref, wc_ref, wn_ref, qn_ref, kvn_ref, wq_ref, wqr_ref, wkv_ref,
                   wv_ref, vone_ref, tab_ref, q_ref, k_ref, v_ref, nq_ref, nk_ref, nv_ref):
    hn = _rms(x_ref[...], g_ref[...]).astype(BF16)
    c = _dot(hn, wc_ref[...])
    n = _dot(hn, wn_ref[...])
    nq_ref[...] = (n[:, :NA_WIDTH] * (NA_HEAD_DIM ** -0.5)).astype(BF16)
    nk_ref[...] = n[:, NA_WIDTH:2 * NA_WIDTH].astype(BF16)
    nv_ref[...] = n[:, 2 * NA_WIDTH:].astype(BF16)
    cqn = _rms(c[:, :MLA_Q_LORA], qn_ref[...]).astype(BF16)
    ckvn = _rms(c[:, MLA_Q_LORA:MLA_Q_LORA + MLA_KV_LORA], kvn_ref[...]).astype(BF16)
    kr = c[:, 384:512]
    krr = c[:, 512:640]
    cosq = tab_ref[:, 0:128]
    sinq = tab_ref[:, 128:256]
    cosk = tab_ref[:, 256:384]
    sink = tab_ref[:, 384:512]
    k_rope = kr * cosk + krr * sink
    q = _dot(cqn, wq_ref[...])
    qr = _dot(cqn, wqr_ref[...])
    kn = _dot(ckvn, wkv_ref[...])
    for h in range(MLA_HEADS):
        sl = slice(h * LANES, (h + 1) * LANES)
        q_ref[:, sl] = (q[:, sl] * cosq + qr[:, sl] * sinq).astype(BF16)
        k_ref[:, sl] = (kn[:, sl] + k_rope).astype(BF16)
    v_ref[...] = (_dot(ckvn, wv_ref[...]) + vone_ref[...]).astype(BF16)


def _front0(x2, g, a_w_in, a_q_norm, a_w_q_up, a_kv_norm, a_w_kv_up, seq):
    t_rows = x2.shape[0]
    tm = ROW_TILE
    n_seq = seq // tm
    w = a_w_in
    wkr = w[:, 384:416]
    zeros = lambda n: jnp.zeros((D_MODEL, n), F32)
    wkr_full = jnp.concatenate([zeros(64), wkr, zeros(32)], axis=1)
    wkr_rot = jnp.concatenate([zeros(64), -wkr[:, 16:], wkr[:, :16], zeros(32)], axis=1)
    wc = jnp.concatenate([w[:, :384], wkr_full, wkr_rot], axis=1).astype(BF16)
    wn = w[:, 416:].astype(BF16)
    wq3 = a_w_q_up.reshape(MLA_Q_LORA, MLA_HEADS, MLA_NOPE + MLA_ROPE)
    nope, rope = wq3[..., :MLA_NOPE], wq3[..., MLA_NOPE:]
    pad32 = jnp.zeros((MLA_Q_LORA, MLA_HEADS, 32), F32)
    pad64 = jnp.zeros((MLA_Q_LORA, MLA_HEADS, 64), F32)
    wq = jnp.concatenate([nope, rope, pad32], axis=-1).reshape(MLA_Q_LORA, -1).astype(BF16)
    wqr = jnp.concatenate([pad64, -rope[..., 16:], rope[..., :16], pad32],
                          axis=-1).reshape(MLA_Q_LORA, -1).astype(BF16)
    wkv3 = a_w_kv_up.reshape(MLA_KV_LORA, MLA_HEADS, MLA_NOPE + MLA_V)
    knope, vup = wkv3[..., :MLA_NOPE], wkv3[..., MLA_NOPE:]
    kpad = jnp.zeros((MLA_KV_LORA, MLA_HEADS, 64), F32)
    wkv = jnp.concatenate([knope, kpad], axis=-1).reshape(MLA_KV_LORA, -1).astype(BF16)
    vup4 = vup.reshape(MLA_KV_LORA, MLA_HEADS // 2, 2, MLA_V)
    vpad = jnp.zeros((MLA_KV_LORA, MLA_HEADS // 2, 64), F32)
    wv = jnp.concatenate([vup4[:, :, 0], vpad, vpad, vup4[:, :, 1]], axis=-1)
    wv = wv.reshape(MLA_KV_LORA, -1).astype(BF16)
    pair_one = np.zeros((256,), np.float32)
    pair_one[64] = 1.0
    pair_one[128] = 1.0
    vone = jnp.asarray(np.tile(pair_one, MLA_HEADS // 2)[None, :])
    cos, sin = _rope_tables(seq, MLA_ROPE)
    sc = (MLA_NOPE + MLA_ROPE) ** -0.5 * math.log2(math.e)
    z16 = jnp.zeros((seq, 32), F32)
    z64 = jnp.zeros((seq, 64), F32)
    tab = jnp.concatenate([
        jnp.full((seq, 64), sc, F32), cos * sc, cos * sc, z16,
        z64, sin * sc, sin * sc, z16,
        z64, cos, cos, z16,
        z64, sin, sin, z16], axis=1)
    out_shapes = [jax.ShapeDtypeStruct((t_rows, 1024), BF16)] * 3 + \
                 [jax.ShapeDtypeStruct((t_rows, NA_WIDTH), BF16)] * 3
    row = lambda n: pl.BlockSpec((tm, n), lambda i: (i, 0))
    return pl.pallas_call(
        _front0_kernel,
        grid=(t_rows // tm,),
        in_specs=[row(D_MODEL), _const_spec((1, D_MODEL)), _const_spec(wc.shape), _const_spec(wn.shape),
                  _const_spec((1, MLA_Q_LORA)), _const_spec((1, MLA_KV_LORA)), _const_spec(wq.shape),
                  _const_spec(wqr.shape), _const_spec(wkv.shape), _const_spec(wv.shape),
                  _const_spec((1, 1024)),
                  pl.BlockSpec((tm, 512), lambda i: (i % n_seq, 0))],
        out_specs=[row(1024), row(1024), row(1024), row(NA_WIDTH), row(NA_WIDTH), row(NA_WIDTH)],
        out_shape=out_shapes,
        compiler_params=_cparams(("parallel",)),
        name="front0",
    )(x2, g[None, :], wc, wn, a_q_norm[None, :], a_kv_norm[None, :], wq, wqr, wkv, wv, vone, tab)


def _mla_kernel(q_ref, k_ref, v_ref, o_ref, s_ref):
    tq = q_ref.shape[0]
    seq = k_ref.shape[0]
    n_chunks = seq // MLA_K_CHUNK
    accs = []
    for hh in range(2):
        hs = slice(hh * LANES, (hh + 1) * LANES)
        qh = q_ref[:, hs]
        m = jnp.full((tq, LANES), -jnp.inf, F32)
        for c in range(n_chunks):
            cs = slice(c * MLA_K_CHUNK, (c + 1) * MLA_K_CHUNK)
            s = _dot_nt(qh, k_ref[cs, hs])
            s_ref[:, cs] = s
            for j in range(MLA_K_CHUNK // LANES):
                m = jnp.maximum(m, s[:, j * LANES:(j + 1) * LANES])
        mrow = jnp.max(m, axis=-1, keepdims=True)
        acc = jnp.zeros((tq, LANES), F32)
        for c in range(n_chunks):
            cs = slice(c * MLA_K_CHUNK, (c + 1) * MLA_K_CHUNK)
            p = jnp.exp2(s_ref[:, cs] - mrow).astype(BF16)
            acc = acc + _dot(p, v_ref[cs, hs])
        accs.append(acc)
    l0 = accs[0][:, 64:65]
    l1 = accs[1][:, 0:1]
    lane = lax.broadcasted_iota(jnp.int32, (tq, LANES), 1)
    o_ref[...] = jnp.where(lane < 64, accs[0] / l0, accs[1] / l1).astype(BF16)


def _mla_attention(q, k, v2, batch, seq):
    tq = MLA_Q_TILE
    nq = seq // tq
    return pl.pallas_call(
        _mla_kernel,
        grid=(batch, MLA_HEADS // 2, nq),
        in_specs=[pl.BlockSpec((tq, 256), lambda b, p, i: (b * nq + i, p)),
                  pl.BlockSpec((seq, 256), lambda b, p, i: (b, p)),
                  pl.BlockSpec((seq, 256), lambda b, p, i: (b, p))],
        out_specs=pl.BlockSpec((tq, LANES), lambda b, p, i: (b * nq + i, p)),
        out_shape=jax.ShapeDtypeStruct((batch * seq, MLA_HEADS * MLA_V), BF16),
        scratch_shapes=[pltpu.VMEM((tq, seq), F32)],
        compiler_params=_cparams(("parallel", "parallel", "arbitrary")),
        name="mla_attention",
    )(q, k, v2)


def _na_bias_table(rpb, rows):
    kh, kw = NA_WIN_H, NA_WIN_W
    neg = -1e30
    c = np.arange(GRID_W)[:, None]
    kc = np.arange(GRID_W)[None, :]
    cst = np.clip(c - kw // 2, 0, GRID_W - kw)
    col_valid = (kc >= cst) & (kc < cst + kw)
    col_off = kc - c + (kw - 1)
    onehot = ((col_off[None] == np.arange(2 * kw - 1)[:, None, None]) & col_valid[None]).astype(np.float32)
    blocks = jnp.einsum('hdj,jck->hdck', rpb.astype(F32), jnp.asarray(onehot),
                        precision=lax.Precision.HIGHEST)
    blocks = jnp.where(col_valid, blocks, neg)
    masked = jnp.full((NA_HEADS, GRID_W, GRID_W), neg, F32)
    tables = []
    for r0, ws in ((0, 0), (2 * NA_ROWS, 2 * NA_ROWS - kh // 2), (rows - NA_ROWS, rows - NA_KEY_ROWS)):
        q_rows = []
        for ri in range(NA_ROWS):
            r = r0 + ri
            rs = min(max(r - kh // 2, 0), rows - kh)
            k_blocks = []
            for i in range(NA_KEY_ROWS):
                kr = ws + i
                k_blocks.append(blocks[:, kr - r + (kh - 1)] if rs <= kr < rs + kh else masked)
            q_rows.append(jnp.concatenate(k_blocks, axis=-1))
        tables.append(jnp.concatenate(q_rows, axis=1))
    return jnp.stack(tables)


def _na_kernel(q_ref, k_ref, v_ref, b_ref, o_ref):
    g = pl.program_id(1)
    rows = k_ref.shape[0] // GRID_W
    ws = jnp.clip(g * NA_ROWS - NA_WIN_H // 2, 0, rows - NA_KEY_ROWS)
    start = pl.multiple_of(ws * GRID_W, GRID_W)
    nkeys = NA_KEY_ROWS * GRID_W
    nqry = NA_ROWS * GRID_W
    lane = lax.broadcasted_iota(jnp.int32, (1, LANES), 1)
    for p in range(NA_HEADS // 2):
        ps = slice(p * LANES, (p + 1) * LANES)
        q2 = q_ref[:, ps]
        k2 = k_ref[pl.ds(start, nkeys), ps]
        v2 = v_ref[pl.ds(start, nkeys), ps]
        o = jnp.zeros((nqry, LANES), F32)
        for hh in range(2):
            mask = (lane < 64) if hh == 0 else (lane >= 64)
            qm = jnp.where(mask, q2, jnp.zeros_like(q2))
            vm = jnp.where(mask, v2, jnp.zeros_like(v2))
            s = _dot_nt(qm, k2) + b_ref[0, 2 * p + hh]
            m = jnp.max(s, axis=-1, keepdims=True)
            e = jnp.exp(s - m)
            l = jnp.sum(e, axis=-1, keepdims=True)
            o = o + _dot(e.astype(BF16), vm) / l
        o_ref[:, ps] = o.astype(BF16)


def _na_attention(nq, nk, nv, rpb, batch, seq):
    rows = seq // GRID_W
    groups = rows // NA_ROWS
    nqry = NA_ROWS * GRID_W
    nkeys = NA_KEY_ROWS * GRID_W
    bias = _na_bias_table(rpb, rows)
    case = lambda g: (g > 0).astype(jnp.int32) + (g == groups - 1).astype(jnp.int32)
    return pl.pallas_call(
        _na_kernel,
        grid=(batch, groups),
        in_specs=[pl.BlockSpec((nqry, NA_WIDTH), lambda b, g: (b * groups + g, 0)),
                  pl.BlockSpec((seq, NA_WIDTH), lambda b, g: (b, 0)),
                  pl.BlockSpec((seq, NA_WIDTH), lambda b, g: (b, 0)),
                  pl.BlockSpec((1, NA_HEADS, nqry, nkeys), lambda b, g: (case(g), 0, 0, 0))],
        out_specs=pl.BlockSpec((nqry, NA_WIDTH), lambda b, g: (b * groups + g, 0)),
        out_shape=jax.ShapeDtypeStruct((batch * seq, NA_WIDTH), BF16),
        compiler_params=_cparams(("parallel", "arbitrary")),
        name="na_attention",
    )(nq, nk, nv, bias)


def _outproj_kernel(a_ref, b_ref, wa_ref, wb_ref, x_ref, g_ref, o_ref):
    m = _dot(a_ref[...].astype(BF16), wa_ref[...]) + _dot(b_ref[...].astype(BF16), wb_ref[...])
    o_ref[...] = x_ref[...] + _rms(m, g_ref[...])


def _outproj(a, b, w_out, x2, g):
    t_rows = x2.shape[0]
    tm = ROW_TILE
    half = w_out.shape[0] // 2
    wa = w_out[:half].astype(BF16)
    wb = w_out[half:].astype(BF16)
    row = lambda n: pl.BlockSpec((tm, n), lambda i: (i, 0))
    return pl.pallas_call(
        _outproj_kernel,
        grid=(t_rows // tm,),
        in_specs=[row(half), row(half), _const_spec(wa.shape), _const_spec(wb.shape), row(D_MODEL),
                  _const_spec((1, D_MODEL))],
        out_specs=row(D_MODEL),
        out_shape=jax.ShapeDtypeStruct((t_rows, D_MODEL), F32),
        compiler_params=_cparams(("parallel",)),
        name="outproj",
    )(a, b, wa, wb, x2, g[None, :])


def _gelu_tanh(x):
    return 0.5 * x * (1.0 + jnp.tanh(math.sqrt(2.0 / math.pi) * (x + 0.044715 * (x * x * x))))


def _ffn_kernel(x_ref, xp_ref, xn_ref, gpre_ref, wg_ref, wu_ref, cw_ref, wd_ref, gpost_ref, o_ref,
                a_ref, *, n_seq):
    i = pl.program_id(0)
    tm = x_ref.shape[0]
    x = x_ref[...]
    gpre = gpre_ref[...]
    hn = _rms(x, gpre)
    keep_prev = jnp.where(i % n_seq == 0, 0.0, 1.0)
    keep_next = jnp.where(i % n_seq == n_seq - 1, 0.0, 1.0)
    hp = _rms(xp_ref[...], gpre) * keep_prev
    hx = _rms(xn_ref[...], gpre) * keep_next
    h_ext = jnp.concatenate([hp, hn, hx], axis=0).astype(BF16)
    hb = hn.astype(BF16)
    ext = tm + 2 * HALO
    for j in range(D_FF // FFN_CHUNK):
        cs = slice(j * FFN_CHUNK, (j + 1) * FFN_CHUNK)
        gate = _dot(h_ext, wg_ref[:, cs])
        up = _dot(hb, wu_ref[:, cs])
        g_prev = pltpu.roll(gate, 1, 0)[HALO:HALO + tm]
        g_next = pltpu.roll(gate, ext - 1, 0)[HALO:HALO + tm]
        conv = g_prev * cw_ref[0:1, cs] + gate[HALO:HALO + tm] * cw_ref[1:2, cs] + g_next * cw_ref[2:3, cs]
        a_ref[:, cs] = (_gelu_tanh(conv) * up).astype(BF16)
    f = _dot(a_ref[...], wd_ref[...])
    o_ref[...] = x + _rms(f, gpost_ref[...])


def _ffn(x2, gpre, w_gate, w_up, conv_w, w_down, gpost, seq):
    t_rows = x2.shape[0]
    tm = ROW_TILE
    n_seq = seq // tm
    hb = tm // HALO
    n_halo = t_rows // HALO
    row = pl.BlockSpec((tm, D_MODEL), lambda i: (i, 0))
    prev = pl.BlockSpec((HALO, D_MODEL), lambda i: (jnp.maximum(i * hb - 1, 0), 0))
    nxt = pl.BlockSpec((HALO, D_MODEL), lambda i: (jnp.minimum((i + 1) * hb, n_halo - 1), 0))
    return pl.pallas_call(
        functools.partial(_ffn_kernel, n_seq=n_seq),
        grid=(t_rows // tm,),
        in_specs=[row, prev, nxt, _const_spec((1, D_MODEL)), _const_spec((D_MODEL, D_FF)),
                  _const_spec((D_MODEL, D_FF)), _const_spec((3, D_FF)), _const_spec((D_FF, D_MODEL)),
                  _const_spec((1, D_MODEL))],
        out_specs=row,
        out_shape=jax.ShapeDtypeStruct((t_rows, D_MODEL), F32),
        scratch_shapes=[pltpu.VMEM((tm, D_FF), BF16)],
        compiler_params=_cparams(("parallel",)),
        name="ffn",
    )(x2, x2, x2, gpre[None, :], w_gate.astype(BF16), w_up.astype(BF16), conv_w,
      w_down.astype(BF16), gpost[None, :])


def _front1_kernel(x_ref, g_ref, w_ref, tab_ref, rq_ref, rk_ref, rv_ref, rg_ref, hy_ref):
    hn = _rms(x_ref[...], g_ref[...]).astype(BF16)
    r = _dot(hn, w_ref[:, :4 * RET_W])
    cosr = tab_ref[:, 0:128]
    sinr = tab_ref[:, 128:256]
    kscale = RET_DIM ** -0.5
    for h in range(RET_HEADS):
        qs = slice(h * RET_DIM, (h + 1) * RET_DIM)
        ks = slice(RET_W + h * RET_DIM, RET_W + (h + 1) * RET_DIM)
        qh = r[:, qs]
        kh = r[:, ks]
        rq_ref[:, qs] = (qh * cosr + pltpu.roll(qh, RET_DIM // 2, 1) * sinr).astype(BF16)
        rk_ref[:, qs] = ((kh * cosr + pltpu.roll(kh, RET_DIM // 2, 1) * sinr) * kscale).astype(BF16)
    rv_ref[...] = r[:, 2 * RET_W:3 * RET_W].astype(BF16)
    rg_ref[...] = r[:, 3 * RET_W:]
    hy_ref[...] = _dot(hn, w_ref[:, 4 * RET_W:]).astype(BF16)


def _front1(x2, g, c_w_in, seq):
    t_rows = x2.shape[0]
    tm = ROW_TILE
    n_seq = seq // tm
    cos, sin = _rope_tables(seq, RET_DIM)
    tab = jnp.concatenate([cos, cos, -sin, sin], axis=1)
    w = c_w_in.astype(BF16)
    row = lambda n: pl.BlockSpec((tm, n), lambda i: (i, 0))
    out_shapes = [jax.ShapeDtypeStruct((t_rows, RET_W), BF16)] * 3 + \
                 [jax.ShapeDtypeStruct((t_rows, RET_W), F32),
                  jax.ShapeDtypeStruct((t_rows, 3 * HY_WIDTH), BF16)]
    return pl.pallas_call(
        _front1_kernel,
        grid=(t_rows // tm,),
        in_specs=[row(D_MODEL), _const_spec((1, D_MODEL)), _const_spec(w.shape),
                  pl.BlockSpec((tm, 256), lambda i: (i % n_seq, 0))],
        out_specs=[row(RET_W), row(RET_W), row(RET_W), row(RET_W), row(3 * HY_WIDTH)],
        out_shape=out_shapes,
        compiler_params=_cparams(("parallel",)),
        name="front1",
    )(x2, g[None, :], w, tab)


def _log_sigmoid(x):
    return jnp.minimum(x, 0.0) - jnp.log1p(jnp.exp(-jnp.abs(x)))


def _ret_kernel(q_ref, k_ref, v_ref, g_ref, dec_ref, o_ref, intra_ref, kvf_ref, kvb_ref,
                sf_ref, sb_ref):
    seq = q_ref.shape[0]
    c = RET_CHUNK
    n_chunks = seq // c
    lf = _log_sigmoid(dec_ref[0, 0:1, :])
    lb = _log_sigmoid(dec_ref[0, 1:2, :])
    t = lax.broadcasted_iota(jnp.int32, (c, c), 0).astype(F32)
    s_idx = lax.broadcasted_iota(jnp.int32, (c, c), 1).astype(F32)
    diff = t - s_idx
    dmat = jnp.where(diff >= 0, jnp.exp(jnp.maximum(diff, 0.0) * lf),
                     jnp.exp(jnp.maximum(-diff, 0.0) * lb))
    xi_f = jnp.exp((t + 1.0) * lf)
    xi_b = jnp.exp((c - t) * lb)
    zeta_f = jnp.exp((c - 1.0 - t) * lf)
    zeta_b = jnp.exp(t * lb)
    g_f = jnp.exp(c * lf)
    g_b = jnp.exp(c * lb)

    def chunk_a(n, carry):
        rs = pl.ds(pl.multiple_of(n * c, c), c)
        qn = q_ref[rs, :]
        kn = k_ref[rs, :]
        vn = v_ref[rs, :]
        sc = (_dot_nt(qn, kn) * dmat).astype(BF16)
        intra_ref[rs, :] = _dot(sc, vn)
        knf = kn.astype(F32)
        kvf_ref[rs, :] = _dot((knf * zeta_f).T.astype(BF16), vn)
        kvb_ref[rs, :] = _dot((knf * zeta_b).T.astype(BF16), vn)
        return carry

    lax.fori_loop(0, n_chunks, chunk_a, 0, unroll=RET_UNROLL)

    def scan_f(n, st):
        rs = pl.ds(pl.multiple_of(n * c, c), c)
        sf_ref[rs, :] = st.astype(BF16)
        return g_f * st + kvf_ref[rs, :]

    lax.fori_loop(0, n_chunks, scan_f, jnp.zeros((c, c), F32))

    def scan_b(m, st):
        n = n_chunks - 1 - m
        rs = pl.ds(pl.multiple_of(n * c, c), c)
        sb_ref[rs, :] = st.astype(BF16)
        return g_b * st + kvb_ref[rs, :]

    lax.fori_loop(0, n_chunks, scan_b, jnp.zeros((c, c), F32))

    def chunk_c(n, carry):
        rs = pl.ds(pl.multiple_of(n * c, c), c)
        qn = q_ref[rs, :]
        o = intra_ref[rs, :] + _dot(qn, sf_ref[rs, :]) * xi_f + _dot(qn, sb_ref[rs, :]) * xi_b
        o = o * lax.rsqrt(jnp.mean(o * o, axis=-1, keepdims=True) + EPS)
        gate = g_ref[rs, :]
        o_ref[rs, :] = (o * (gate * jax.nn.sigmoid(gate))).astype(BF16)
        return carry

    lax.fori_loop(0, n_chunks, chunk_c, 0, unroll=RET_UNROLL)


def _retention(rq, rk, rv, rg, decay_fwd, decay_bwd, batch, seq):
    dec = jnp.stack([decay_fwd, decay_bwd], axis=1)
    dec = jnp.broadcast_to(dec[:, :, None], (RET_HEADS, 2, LANES)).astype(F32)
    blk = pl.BlockSpec((seq, RET_DIM), lambda b, h: (b, h))
    return pl.pallas_call(
        _ret_kernel,
        grid=(batch, RET_HEADS),
        in_specs=[blk, blk, blk, blk, pl.BlockSpec((1, 2, LANES), lambda b, h: (h, 0, 0))],
        out_specs=blk,
        out_shape=jax.ShapeDtypeStruct((batch * seq, RET_W), BF16),
        scratch_shapes=[pltpu.VMEM((seq, RET_DIM), F32), pltpu.VMEM((seq, RET_DIM), F32),
                        pltpu.VMEM((seq, RET_DIM), F32), pltpu.VMEM((seq, RET_DIM), BF16),
                        pltpu.VMEM((seq, RET_DIM), BF16)],
        compiler_params=_cparams(("parallel", "parallel")),
        name="retention",
    )(rq, rk, rv, rg, dec)


def _fft_tables():
    n = FFT_N1 * FFT_N2
    n2 = np.arange(FFT_N2)[:, None, None]
    k1 = np.arange(FFT_N1)[None, :, None]
    n1 = np.arange(FFT_N1 // 2)[None, None, :]
    ang = 2.0 * np.pi * ((k1 * (FFT_N2 * n1 + n2)) % n) / n
    c1, s1 = np.cos(ang), np.sin(ang)
    f1 = np.concatenate([c1, -s1], axis=1)
    f1c = np.concatenate([np.concatenate([c1, s1], axis=2),
                          np.concatenate([-s1, c1], axis=2)], axis=1)
    c3, s3 = c1.transpose(0, 2, 1), s1.transpose(0, 2, 1)
    g3c = np.concatenate([np.concatenate([c3, -s3], axis=2),
                          np.concatenate([s3, c3], axis=2)], axis=1)
    a = np.arange(FFT_N2)
    ang2 = 2.0 * np.pi * ((a[:, None] * a[None, :]) % FFT_N2) / FFT_N2
    cc, ss = np.cos(ang2), np.sin(ang2)
    g2 = np.block([[cc, ss], [-ss, cc]])
    g2i = np.block([[cc, -ss], [ss, cc]])
    return dict(f1=f1, f1c=f1c, g2=g2, g2i=g2i, g3c=g3c)


def _split_bf16(a):
    hi = jnp.asarray(a, F32).astype(BF16)
    lo = (jnp.asarray(a, F32) - hi.astype(F32)).astype(BF16)
    return hi, lo


def _dot3(a_hi, a_lo, x):
    x_hi = x.astype(BF16)
    x_lo = (x - x_hi.astype(F32)).astype(BF16)
    return _dot(a_hi, x_hi) + _dot(a_hi, x_lo) + _dot(a_lo, x_hi)


def _hymlp_kernel(z_ref, w1_ref, b1_ref, w2_ref, b2_ref, w3_ref, b3_ref, fr_ref, h_ref):
    hp = lax.Precision.HIGHEST
    fr = fr_ref[...]

    def dense(h, w_ref, b_ref):
        return jnp.sin(fr * (jnp.dot(h, w_ref[...], precision=hp, preferred_element_type=F32) + b_ref[...]))

    h_ref[...] = dense(dense(dense(z_ref[...], w1_ref, b1_ref), w2_ref, b2_ref), w3_ref, b3_ref)


def _hyfilt_kernel(h_ref, w4f_ref, w4b_ref, dl_ref, f1h_ref, f1l_ref, g2h_ref, g2l_ref, kf_ref,
                   t_ref, a_ref):
    seq = h_ref.shape[0]
    hp = lax.Precision.HIGHEST
    blk = FFT_N2
    rows = HY_MLP_ROWS
    inv_n = 1.0 / (FFT_N1 * FFT_N2)
    dl = jnp.abs(dl_ref[...])

    for w4_ref, backward in ((w4f_ref, False), (w4b_ref, True)):

        def fill(i, carry):
            row0 = pl.multiple_of(i * rows, rows)
            t_idx = row0 + lax.broadcasted_iota(jnp.int32, (rows, LANES), 0)
            window = jnp.exp(-(t_idx.astype(F32) / (seq - 1)) * dl)
            filt = jnp.dot(h_ref[pl.ds(row0, rows), :], w4_ref[...], precision=hp,
                           preferred_element_type=F32) * window
            if backward:
                filt = jnp.where(t_idx == 0, 0.0, filt)
            for q in range(rows // blk):
                dst = pl.multiple_of((i * (rows // blk) + q) * PITCH_T, 8)
                t_ref[pl.ds(dst, blk), :] = filt[q * blk:(q + 1) * blk]
            return carry

        lax.fori_loop(0, seq // rows, fill, 0)

        def stage1(n2, carry):
            xs = t_ref[pl.ds(n2, FFT_N1 // 2, stride=PITCH_T), :]
            r = _dot3(f1h_ref[n2], f1l_ref[n2], xs)
            a_ref[pl.ds(n2, FFT_N1, stride=PITCH_A), :] = r[:FFT_N1]
            a_ref[pl.ds(FFT_N2 + n2, FFT_N1, stride=PITCH_A), :] = r[FFT_N1:]
            return carry

        lax.fori_loop(0, FFT_N2, stage1, 0, unroll=FFT_UNROLL)

        def stage2(k1, carry):
            rs = pl.ds(pl.multiple_of(k1 * PITCH_A, 8), 2 * FFT_N2)
            y = _dot3(g2h_ref[...], g2l_ref[...], a_ref[rs, :]) * inv_n
            if backward:
                kf_ref[k1] = kf_ref[k1] + jnp.concatenate([y[:FFT_N2], -y[FFT_N2:]], axis=0)
            else:
                kf_ref[k1] = y
            return carry

        lax.fori_loop(0, FFT_N1, stage2, 0, unroll=FFT_UNROLL)


def _hyena_filter_spectrum(w1, b1, w2, b2, w3, b3, w4, freq, seq, tables):
    f1, g2 = tables["f1"], tables["g2"]
    t = jnp.arange(seq, dtype=F32) / (seq - 1)
    bands = (HY_EMB_DIM - 1) // 2
    w = 2.0 * math.pi * jnp.arange(seq, dtype=F32) / seq
    f = jnp.linspace(1e-4, bands - 1, bands, dtype=F32)
    fw = f[None, :] * w[:, None]
    z = jnp.concatenate([t[:, None], jnp.cos(fw), -jnp.sin(fw)], axis=-1)
    z = jnp.pad(z, ((0, 0), (0, LANES - HY_EMB_DIM)))
    w1p = jnp.pad(w1, ((0, LANES - HY_EMB_DIM), (0, 0)))
    max_decay = math.log(HY_TARGET) / HY_FAST_DECAY
    min_decay = math.log(HY_TARGET) / HY_SLOW_DECAY
    deltas = jnp.linspace(min_decay, max_decay, HY_WIDTH, dtype=F32)[None, :]
    f1h, f1l = _split_bf16(f1)
    g2h, g2l = _split_bf16(g2)
    n_ct = HY_WIDTH // LANES
    hid = HY_FILTER_HIDDEN
    rows_t = (FFT_N1 // 2) * PITCH_T
    rows_a = FFT_N1 * PITCH_A
    h3 = pl.pallas_call(
        _hymlp_kernel,
        grid=(seq // HY_MLP_ROWS,),
        in_specs=[pl.BlockSpec((HY_MLP_ROWS, LANES), lambda i: (i, 0)), _const_spec((LANES, hid)),
                  _const_spec((1, hid)), _const_spec((hid, hid)), _const_spec((1, hid)),
                  _const_spec((hid, hid)), _const_spec((1, hid)), _const_spec((1, hid))],
        out_specs=pl.BlockSpec((HY_MLP_ROWS, hid), lambda i: (i, 0)),
        out_shape=jax.ShapeDtypeStruct((seq, hid), F32),
        compiler_params=_cparams(("parallel",)),
        name="hyena_mlp",
    )(z, w1p, b1[None, :], w2, b2[None, :], w3, b3[None, :], freq[None, :])
    return pl.pallas_call(
        _hyfilt_kernel,
        grid=(n_ct,),
        in_specs=[_const_spec((seq, hid)),
                  pl.BlockSpec((hid, LANES), lambda c: (0, c)),
                  pl.BlockSpec((hid, LANES), lambda c: (0, n_ct + c)),
                  pl.BlockSpec((1, LANES), lambda c: (0, c)),
                  _const_spec(f1h.shape), _const_spec(f1l.shape), _const_spec(g2h.shape),
                  _const_spec(g2l.shape)],
        out_specs=pl.BlockSpec((FFT_N1, 2 * FFT_N2, LANES), lambda c: (0, 0, c)),
        out_shape=jax.ShapeDtypeStruct((FFT_N1, 2 * FFT_N2, HY_WIDTH), F32),
        scratch_shapes=[pltpu.VMEM((rows_t, LANES), F32), pltpu.VMEM((rows_a, LANES), F32)],
        compiler_params=_cparams(("arbitrary",)),
        name="hyena_filter",
    )(h3, w4, w4, deltas, f1h, f1l, g2h, g2l)


def _hyconv_kernel(x0_ref, x1_ref, v_ref, c0_ref, c1_ref, cv_ref, bias_ref, kf_ref, f1_ref, g2_ref,
                   g2i_ref, g3_ref, o_ref, ua_ref, ub_ref, a_ref, b_ref):
    seq = x0_ref.shape[1]
    blk = FFT_N2
    half = FFT_N1 // 2
    pad = BF16_SUBLANES
    batches = tuple((x0_ref.at[s], x1_ref.at[s], v_ref.at[s], u_ref, o_ref.at[s])
                    for s, u_ref in enumerate((ua_ref, ub_ref)))

    def dwconv(ref, w_ref, start, nrows, off, zero_first, zero_last):
        win = ref[pl.ds(start, nrows), :].astype(F32)
        idx = lax.broadcasted_iota(jnp.int32, (nrows, LANES), 0)
        prev = pltpu.roll(win, 1, 0)
        nxt = pltpu.roll(win, nrows - 1, 0)
        if zero_first:
            prev = jnp.where(idx == 0, 0.0, prev)
        if zero_last:
            nxt = jnp.where(idx == nrows - 1, 0.0, nxt)
        out = prev * w_ref[0:1, :] + win * w_ref[1:2, :] + nxt * w_ref[2:3, :]
        return out[off:off + blk]

    def fill(n1, start, nrows, off, zero_first=False, zero_last=False):
        args = (start, nrows, off, zero_first, zero_last)
        for x0s, x1s, vs, u_ref, os in batches:
            u = dwconv(vs, cv_ref, *args) * dwconv(x1s, c1_ref, *args)
            u_ref[pl.ds(_aligned(n1 * PITCH_T, 8), blk), :] = u
            os[pl.ds(_aligned(n1 * blk, blk), blk), :] = dwconv(x0s, c0_ref, *args).astype(BF16)

    fill(0, 0, blk + pad, 0, zero_first=True)

    def fill_mid(n1, carry):
        fill(n1, _aligned(n1 * blk - pad, pad), blk + 2 * pad, pad)
        return carry

    lax.fori_loop(1, half - 1, fill_mid, 0, unroll=2)
    fill(half - 1, seq - blk - pad, blk + pad, pad, zero_last=True)

    def stage1(n2, carry):
        xs = jnp.concatenate([ua_ref[pl.ds(n2, half, stride=PITCH_T), :],
                              ub_ref[pl.ds(n2, half, stride=PITCH_T), :]], axis=0).astype(BF16)
        r = _dot(f1_ref[n2], xs)
        a_ref[pl.ds(n2, FFT_N1, stride=PITCH_A), :] = r[:FFT_N1]
        a_ref[pl.ds(FFT_N2 + n2, FFT_N1, stride=PITCH_A), :] = r[FFT_N1:]
        return carry

    lax.fori_loop(0, FFT_N2, stage1, 0, unroll=FFT_UNROLL)

    def stage2(j, carry):
        k1s = [j * FFT_GROUP + g for g in range(FFT_GROUP)]
        w = jnp.concatenate([a_ref[pl.ds(pl.multiple_of(k1 * PITCH_A, 8), 2 * FFT_N2), :].astype(BF16)
                             for k1 in k1s], axis=1)
        y = _dot(g2_ref[...], w)
        kf = jnp.concatenate([kf_ref[k1] for k1 in k1s], axis=1)
        yr, yi = y[:FFT_N2], y[FFT_N2:]
        kr, ki = kf[:FFT_N2], kf[FFT_N2:]
        z = jnp.concatenate([yr * kr - yi * ki, yr * ki + yi * kr], axis=0).astype(BF16)
        bv = _dot(g2i_ref[...], z)
        for g, k1 in enumerate(k1s):
            ls = slice(g * LANES, (g + 1) * LANES)
            b_ref[pl.ds(k1, FFT_N2, stride=PITCH_B), :] = bv[:FFT_N2, ls]
            b_ref[pl.ds(FFT_N1 + k1, FFT_N2, stride=PITCH_B), :] = bv[FFT_N2:, ls]
        return carry

    lax.fori_loop(0, FFT_N1 // FFT_GROUP, stage2, 0, unroll=FFT_UNROLL // FFT_GROUP)
    bias = bias_ref[...]

    def stage3(n2, carry):
        rs = pl.ds(pl.multiple_of(n2 * PITCH_B, 8), 2 * FFT_N1)
        y = _dot(g3_ref[n2], b_ref[rs, :].astype(BF16))
        ts = pl.ds(n2, half, stride=PITCH_T)
        ua_ref[ts, :] = y[:half] + ua_ref[ts, :] * bias
        ub_ref[ts, :] = y[half:] + ub_ref[ts, :] * bias
        return carry

    lax.fori_loop(0, FFT_N2, stage3, 0, unroll=FFT_UNROLL)

    def finish(n1, carry):
        src = pl.ds(pl.multiple_of(n1 * PITCH_T, 8), blk)
        dst = pl.ds(pl.multiple_of(n1 * blk, blk), blk)
        for _, _, _, u_ref, os in batches:
            os[dst, :] = (u_ref[src, :] * os[dst, :].astype(F32)).astype(BF16)
        return carry

    lax.fori_loop(0, half, finish, 0, unroll=FFT_UNROLL)


def _hyena_conv(hy, short_conv, hy_bias, kf, batch, seq, tables):
    n_ct = HY_WIDTH // LANES
    rows_t = (FFT_N1 // 2) * PITCH_T
    col = lambda off: pl.BlockSpec((2, seq, LANES), lambda c, b: (b, 0, off + c))
    cw = lambda off: pl.BlockSpec((3, LANES), lambda c, b: (0, off + c))
    const = lambda name: jnp.asarray(tables[name], F32).astype(BF16)
    f1c, g2, g2i, g3c = const("f1c"), const("g2"), const("g2i"), const("g3c")
    hy3 = hy.reshape(batch, seq, 3 * HY_WIDTH)
    y = pl.pallas_call(
        _hyconv_kernel,
        grid=(n_ct, batch // 2),
        in_specs=[col(0), col(n_ct), col(2 * n_ct), cw(0), cw(n_ct), cw(2 * n_ct),
                  pl.BlockSpec((1, LANES), lambda c, b: (0, c)),
                  pl.BlockSpec((FFT_N1, 2 * FFT_N2, LANES), lambda c, b: (0, 0, c),
                               pipeline_mode=pl.Buffered(1)),
                  _const_spec(f1c.shape), _const_spec(g2.shape), _const_spec(g2i.shape),
                  _const_spec(g3c.shape)],
        out_specs=col(0),
        out_shape=jax.ShapeDtypeStruct((batch, seq, HY_WIDTH), BF16),
        scratch_shapes=[pltpu.VMEM((rows_t, LANES), F32), pltpu.VMEM((rows_t, LANES), F32),
                        pltpu.VMEM((FFT_N1 * PITCH_A, LANES), F32),
                        pltpu.VMEM((FFT_N2 * PITCH_B, LANES), F32)],
        compiler_params=_cparams(("arbitrary", "arbitrary"), HYCONV_VMEM_LIMIT),
        name="hyena_conv",
    )(hy3, hy3, hy3, short_conv, short_conv, short_conv, hy_bias[None, :], kf, f1c, g2, g2i, g3c)
    return y.reshape(batch * seq, HY_WIDTH)


def kernel(x, mix_pre_norm, mix_post_norm, ffn_pre_norm, ffn_post_norm, ffn_w_gate, ffn_w_up, ffn_conv, ffn_w_down, a_w_in, a_q_norm, a_w_q_up, a_kv_norm, a_w_kv_up, a_rpb, a_w_out, c_w_in, c_decay_fwd, c_decay_bwd, c_short_conv, c_filt_w1, c_filt_b1, c_filt_w2, c_filt_b2, c_filt_w3, c_filt_b3, c_filt_w4, c_filt_freq, c_hy_bias, c_w_out):
    batch, seq, d = x.shape
    assert d == D_MODEL and seq * 2 == FFT_N1 * FFT_N2 and seq % ROW_TILE == 0 and batch % 2 == 0
    x2 = x.reshape(batch * seq, d)

    q, k, v2, nq, nk, nv = _front0(x2, mix_pre_norm[0], a_w_in[0], a_q_norm[0], a_w_q_up[0],
                                   a_kv_norm[0], a_w_kv_up[0], seq)
    a = _mla_attention(q, k, v2, batch, seq)
    b = _na_attention(nq, nk, nv, a_rpb[0], batch, seq)
    x2 = _outproj(a, b, a_w_out[0], x2, mix_post_norm[0])
    x2 = _ffn(x2, ffn_pre_norm[0], ffn_w_gate[0], ffn_w_up[0], ffn_conv[0], ffn_w_down[0],
              ffn_post_norm[0], seq)

    rq, rk, rv, rg, hy = _front1(x2, mix_pre_norm[1], c_w_in[0], seq)
    c = _retention(rq, rk, rv, rg, c_decay_fwd[0], c_decay_bwd[0], batch, seq)
    tables = _fft_tables()
    kf = _hyena_filter_spectrum(c_filt_w1[0], c_filt_b1[0], c_filt_w2[0], c_filt_b2[0], c_filt_w3[0],
                                c_filt_b3[0], c_filt_w4[0], c_filt_freq[0], seq, tables)
    dd = _hyena_conv(hy, c_short_conv[0], c_hy_bias[0], kf, batch, seq, tables)
    x2 = _outproj(c, dd, c_w_out[0], x2, mix_post_norm[1])
    x2 = _ffn(x2, ffn_pre_norm[1], ffn_w_gate[1], ffn_w_up[1], ffn_conv[1], ffn_w_down[1],
              ffn_post_norm[1], seq)
    return x2.reshape(batch, seq, d)
```

```python
import functools
import math

import numpy as np
import jax
import jax.numpy as jnp
from jax import lax
from jax.experimental import pallas as pl
from jax.experimental.pallas import tpu as pltpu

F32 = jnp.float32
BF16 = jnp.bfloat16

D_MODEL = 1024
GRID_W = 64
MLA_HEADS = 8
MLA_Q_LORA = 256
MLA_KV_LORA = 128
MLA_NOPE = 64
MLA_ROPE = 32
MLA_V = 64
ROPE_THETA = 10000.0
NA_HEADS = 8
NA_HEAD_DIM = 64
NA_WIN_H = 8
NA_WIN_W = 16
NA_WIDTH = NA_HEADS * NA_HEAD_DIM
RET_HEADS = 4
RET_DIM = 128
RET_CHUNK = 128
RET_W = RET_HEADS * RET_DIM
HY_WIDTH = 512
HY_EMB_DIM = 33
HY_FILTER_HIDDEN = 64
HY_FAST_DECAY = 0.3
HY_SLOW_DECAY = 1.5
HY_TARGET = 1e-2
D_FF = 2816
EPS = 1e-6
LOG2E = math.log2(math.e)

LANES = 128
BF16_SUBLANES = 16
VMEM_LIMIT = 56 * 1024 * 1024
HYCONV_VMEM_LIMIT = 60 * 1024 * 1024

ROW_TILE = 512
MLA_Q_TILE = 512
MLA_SUB_TILE = 256
MLA_K_CHUNK = 512
NA_ROWS = 4
NA_KEY_ROWS = 12
NA_K_CHUNK = 256
FFN_CHUNK = 256
RET_UNROLL = 4
FFT_UNROLL = 4
FFT_GROUP = 4
HY_MLP_ROWS = 256
HALO = BF16_SUBLANES

FFT_N1 = 128
FFT_N2 = 64
PITCH_T = 72
PITCH_A = 136
PITCH_B = 264


def _cparams(sem, vmem_limit=VMEM_LIMIT):
    return pltpu.CompilerParams(dimension_semantics=sem, vmem_limit_bytes=vmem_limit)


def _rms(x, g):
    return x * lax.rsqrt(jnp.mean(x * x, axis=-1, keepdims=True) + EPS) * g


def _dot(a, b):
    return jnp.dot(a, b, preferred_element_type=F32)


def _dot_nt(a, b):
    return lax.dot_general(a, b, (((1,), (1,)), ((), ())), preferred_element_type=F32)


def _aligned(x, m):
    return x if isinstance(x, int) else pl.multiple_of(x, m)


def _const_spec(shape):
    nd = len(shape)
    return pl.BlockSpec(shape, lambda *_: (0,) * nd, pipeline_mode=pl.Buffered(1))


def _rope_tables(length, dim):
    inv = ROPE_THETA ** (-jnp.arange(0, dim, 2, dtype=F32) / dim)
    ang = jnp.arange(length, dtype=F32)[:, None] * inv[None, :]
    return jnp.cos(ang), jnp.sin(ang)


def _front0_kernel(x_ref, g_ref, wc_ref, wn_ref, qn_ref, kvn_ref, wq_ref, wqr_ref, wkv_ref,
                   wv_ref, vone_ref, tab_ref, q_ref, k_ref, v_ref, nq_ref, nk_ref, nv_ref):
    hn = _rms(x_ref[...], g_ref[...]).astype(BF16)
    c = _dot(hn, wc_ref[...])
    n = _dot(hn, wn_ref[...])
    nq_ref[...] = (n[:, :NA_WIDTH] * (NA_HEAD_DIM ** -0.5 * LOG2E)).astype(BF16)
    nk_ref[...] = n[:, NA_WIDTH:2 * NA_WIDTH].astype(BF16)
    nv_ref[...] = n[:, 2 * NA_WIDTH:].astype(BF16)
    cqn = _rms(c[:, :MLA_Q_LORA], qn_ref[...]).astype(BF16)
    ckvn = _rms(c[:, MLA_Q_LORA:MLA_Q_LORA + MLA_KV_LORA], kvn_ref[...]).astype(BF16)
    kr = c[:, 384:512]
    krr = c[:, 512:640]
    cosq = tab_ref[:, 0:128]
    sinq = tab_ref[:, 128:256]
    cosk = tab_ref[:, 256:384]
    sink = tab_ref[:, 384:512]
    k_rope = kr * cosk + krr * sink
    q = _dot(cqn, wq_ref[...])
    qr = _dot(cqn, wqr_ref[...])
    kn = _dot(ckvn, wkv_ref[...])
    for h in range(MLA_HEADS):
        sl = slice(h * LANES, (h + 1) * LANES)
        q_ref[:, sl] = (q[:, sl] * cosq + qr[:, sl] * sinq).astype(BF16)
        k_ref[:, sl] = (kn[:, sl] + k_rope).astype(BF16)
    v_ref[...] = (_dot(ckvn, wv_ref[...]) + vone_ref[...]).astype(BF16)


def _front0(x2, g, a_w_in, a_q_norm, a_w_q_up, a_kv_norm, a_w_kv_up, seq):
    t_rows = x2.shape[0]
    tm = ROW_TILE
    n_seq = seq // tm
    w = a_w_in
    wkr = w[:, 384:416]
    zeros = lambda n: jnp.zeros((D_MODEL, n), F32)
    wkr_full = jnp.concatenate([zeros(64), wkr, zeros(32)], axis=1)
    wkr_rot = jnp.concatenate([zeros(64), -wkr[:, 16:], wkr[:, :16], zeros(32)], axis=1)
    wc = jnp.concatenate([w[:, :384], wkr_full, wkr_rot], axis=1).astype(BF16)
    wn = w[:, 416:].astype(BF16)
    wq3 = a_w_q_up.reshape(MLA_Q_LORA, MLA_HEADS, MLA_NOPE + MLA_ROPE)
    nope, rope = wq3[..., :MLA_NOPE], wq3[..., MLA_NOPE:]
    pad32 = jnp.zeros((MLA_Q_LORA, MLA_HEADS, 32), F32)
    pad64 = jnp.zeros((MLA_Q_LORA, MLA_HEADS, 64), F32)
    wq = jnp.concatenate([nope, rope, pad32], axis=-1).reshape(MLA_Q_LORA, -1).astype(BF16)
    wqr = jnp.concatenate([pad64, -rope[..., 16:], rope[..., :16], pad32],
                          axis=-1).reshape(MLA_Q_LORA, -1).astype(BF16)
    wkv3 = a_w_kv_up.reshape(MLA_KV_LORA, MLA_HEADS, MLA_NOPE + MLA_V)
    knope, vup = wkv3[..., :MLA_NOPE], wkv3[..., MLA_NOPE:]
    kpad = jnp.zeros((MLA_KV_LORA, MLA_HEADS, 64), F32)
    wkv = jnp.concatenate([knope, kpad], axis=-1).reshape(MLA_KV_LORA, -1).astype(BF16)
    vup4 = vup.reshape(MLA_KV_LORA, MLA_HEADS // 2, 2, MLA_V)
    vpad = jnp.zeros((MLA_KV_LORA, MLA_HEADS // 2, 64), F32)
    wv = jnp.concatenate([vup4[:, :, 0], vpad, vpad, vup4[:, :, 1]], axis=-1)
    wv = wv.reshape(MLA_KV_LORA, -1).astype(BF16)
    pair_one = np.zeros((256,), np.float32)
    pair_one[64] = 1.0
    pair_one[128] = 1.0
    vone = jnp.asarray(np.tile(pair_one, MLA_HEADS // 2)[None, :])
    cos, sin = _rope_tables(seq, MLA_ROPE)
    sc = (MLA_NOPE + MLA_ROPE) ** -0.5 * math.log2(math.e)
    z16 = jnp.zeros((seq, 32), F32)
    z64 = jnp.zeros((seq, 64), F32)
    tab = jnp.concatenate([
        jnp.full((seq, 64), sc, F32), cos * sc, cos * sc, z16,
        z64, sin * sc, sin * sc, z16,
        z64, cos, cos, z16,
        z64, sin, sin, z16], axis=1)
    out_shapes = [jax.ShapeDtypeStruct((t_rows, 1024), BF16)] * 3 + \
                 [jax.ShapeDtypeStruct((t_rows, NA_WIDTH), BF16)] * 3
    row = lambda n: pl.BlockSpec((tm, n), lambda i: (i, 0))
    return pl.pallas_call(
        _front0_kernel,
        grid=(t_rows // tm,),
        in_specs=[row(D_MODEL), _const_spec((1, D_MODEL)), _const_spec(wc.shape), _const_spec(wn.shape),
                  _const_spec((1, MLA_Q_LORA)), _const_spec((1, MLA_KV_LORA)), _const_spec(wq.shape),
                  _const_spec(wqr.shape), _const_spec(wkv.shape), _const_spec(wv.shape),
                  _const_spec((1, 1024)),
                  pl.BlockSpec((tm, 512), lambda i: (i % n_seq, 0))],
        out_specs=[row(1024), row(1024), row(1024), row(NA_WIDTH), row(NA_WIDTH), row(NA_WIDTH)],
        out_shape=out_shapes,
        compiler_params=_cparams(("parallel",)),
        name="front0",
    )(x2, g[None, :], wc, wn, a_q_norm[None, :], a_kv_norm[None, :], wq, wqr, wkv, wv, vone, tab)


def _pair_normalise(acc_even, acc_odd):
    lane = lax.broadcasted_iota(jnp.int32, acc_even.shape, 1)
    return jnp.where(lane < 64, acc_even / acc_even[:, 64:65], acc_odd / acc_odd[:, 0:1])


def _staggered(units, bufs, scores, weighted_values):
    accs = {}
    pending = None
    for i, unit in enumerate(units):
        s_ref = bufs[i % len(bufs)]
        mrow = scores(unit, s_ref)
        if pending is not None:
            accs[pending[0]] = weighted_values(*pending)
        pending = (unit, s_ref, mrow)
    accs[pending[0]] = weighted_values(*pending)
    return accs


def _mla_kernel(q_ref, k_ref, v_ref, o_ref, s0_ref, s1_ref):
    sub = MLA_SUB_TILE
    seq = k_ref.shape[0]
    n_sub = q_ref.shape[0] // sub
    chunks = [slice(c * MLA_K_CHUNK, (c + 1) * MLA_K_CHUNK) for c in range(seq // MLA_K_CHUNK)]
    head = lambda hh: slice(hh * LANES, (hh + 1) * LANES)

    def scores(unit, s_ref):
        r, hh = unit
        qh = q_ref[r * sub:(r + 1) * sub, head(hh)]
        m = jnp.full((sub, LANES), -jnp.inf, F32)
        for cs in chunks:
            s = _dot_nt(qh, k_ref[cs, head(hh)])
            s_ref[:, cs] = s
            for j in range(MLA_K_CHUNK // LANES):
                m = jnp.maximum(m, s[:, j * LANES:(j + 1) * LANES])
        return jnp.max(m, axis=-1, keepdims=True)

    def weighted_values(unit, s_ref, mrow):
        _, hh = unit
        acc = jnp.zeros((sub, LANES), F32)
        for cs in chunks:
            p = jnp.exp2(s_ref[:, cs] - mrow).astype(BF16)
            acc = acc + _dot(p, v_ref[cs, head(hh)])
        return acc

    units = [(r, hh) for r in range(n_sub) for hh in range(2)]
    accs = _staggered(units, (s0_ref, s1_ref), scores, weighted_values)
    for r in range(n_sub):
        o_ref[r * sub:(r + 1) * sub, :] = _pair_normalise(accs[(r, 0)], accs[(r, 1)]).astype(BF16)


def _mla_attention(q, k, v2, batch, seq):
    tq = MLA_Q_TILE
    nq = seq // tq
    return pl.pallas_call(
        _mla_kernel,
        grid=(batch, MLA_HEADS // 2, nq),
        in_specs=[pl.BlockSpec((tq, 256), lambda b, p, i: (b * nq + i, p)),
                  pl.BlockSpec((seq, 256), lambda b, p, i: (b, p)),
                  pl.BlockSpec((seq, 256), lambda b, p, i: (b, p))],
        out_specs=pl.BlockSpec((tq, LANES), lambda b, p, i: (b * nq + i, p)),
        out_shape=jax.ShapeDtypeStruct((batch * seq, MLA_HEADS * MLA_V), BF16),
        scratch_shapes=[pltpu.VMEM((MLA_SUB_TILE, seq), F32), pltpu.VMEM((MLA_SUB_TILE, seq), F32)],
        compiler_params=_cparams(("parallel", "parallel", "arbitrary")),
        name="mla_attention",
    )(q, k, v2)


def _na_bias_table(rpb, rows):
    kh, kw = NA_WIN_H, NA_WIN_W
    neg = -1e30
    c = np.arange(GRID_W)[:, None]
    kc = np.arange(GRID_W)[None, :]
    cst = np.clip(c - kw // 2, 0, GRID_W - kw)
    col_valid = (kc >= cst) & (kc < cst + kw)
    col_off = kc - c + (kw - 1)
    onehot = ((col_off[None] == np.arange(2 * kw - 1)[:, None, None]) & col_valid[None]).astype(np.float32)
    blocks = jnp.einsum('hdj,jck->hdck', rpb.astype(F32), jnp.asarray(onehot),
                        precision=lax.Precision.HIGHEST)
    blocks = jnp.where(col_valid, blocks, neg)
    masked = jnp.full((NA_HEADS, GRID_W, GRID_W), neg, F32)
    tables = []
    for r0, ws in ((0, 0), (2 * NA_ROWS, 2 * NA_ROWS - kh // 2), (rows - NA_ROWS, rows - NA_KEY_ROWS)):
        q_rows = []
        for ri in range(NA_ROWS):
            r = r0 + ri
            rs = min(max(r - kh // 2, 0), rows - kh)
            k_blocks = []
            for i in range(NA_KEY_ROWS):
                kr = ws + i
                k_blocks.append(blocks[:, kr - r + (kh - 1)] if rs <= kr < rs + kh else masked)
            q_rows.append(jnp.concatenate(k_blocks, axis=-1))
        tables.append(jnp.concatenate(q_rows, axis=1))
    return jnp.stack(tables)


def _na_kernel(q_ref, k_ref, v_ref, b_ref, o_ref, s0_ref, s1_ref):
    g = pl.program_id(1)
    rows = k_ref.shape[0] // GRID_W
    ws = jnp.clip(g * NA_ROWS - NA_WIN_H // 2, 0, rows - NA_KEY_ROWS)
    start = pl.multiple_of(ws * GRID_W, GRID_W)
    nkeys = NA_KEY_ROWS * GRID_W
    nqry = NA_ROWS * GRID_W
    lane = lax.broadcasted_iota(jnp.int32, (1, LANES), 1)
    pair = lambda p: slice(p * LANES, (p + 1) * LANES)
    chunks = [slice(c * NA_K_CHUNK, (c + 1) * NA_K_CHUNK) for c in range(nkeys // NA_K_CHUNK)]

    def scores(unit, s_ref):
        p, hh = unit
        q2 = q_ref[:, pair(p)]
        own = (lane < 64) if hh == 0 else (lane >= 64)
        qm = jnp.where(own, q2, jnp.zeros_like(q2))
        m = jnp.full((nqry, LANES), -jnp.inf, F32)
        for cs in chunks:
            k2 = k_ref[pl.ds(pl.multiple_of(start + cs.start, GRID_W), NA_K_CHUNK), pair(p)]
            s = _dot_nt(qm, k2) + b_ref[0, 2 * p + hh, :, cs]
            s_ref[:, cs] = s
            for j in range(NA_K_CHUNK // LANES):
                m = jnp.maximum(m, s[:, j * LANES:(j + 1) * LANES])
        return jnp.max(m, axis=-1, keepdims=True)

    def weighted_values(unit, s_ref, mrow):
        p, hh = unit
        own = (lane < 64) if hh == 0 else (lane >= 64)
        ones_lane = 64 if hh == 0 else 0
        acc = jnp.zeros((nqry, LANES), F32)
        for cs in chunks:
            v2 = v_ref[pl.ds(pl.multiple_of(start + cs.start, GRID_W), NA_K_CHUNK), pair(p)]
            vm = jnp.where(own, v2, jnp.where(lane == ones_lane, 1.0, 0.0).astype(BF16))
            acc = acc + _dot(jnp.exp2(s_ref[:, cs] - mrow).astype(BF16), vm)
        return acc

    units = [(p, hh) for p in range(NA_HEADS // 2) for hh in range(2)]
    accs = _staggered(units, (s0_ref, s1_ref), scores, weighted_values)
    for p in range(NA_HEADS // 2):
        o_ref[:, pair(p)] = _pair_normalise(accs[(p, 0)], accs[(p, 1)]).astype(BF16)


def _na_attention(nq, nk, nv, rpb, batch, seq):
    rows = seq // GRID_W
    groups = rows // NA_ROWS
    nqry = NA_ROWS * GRID_W
    nkeys = NA_KEY_ROWS * GRID_W
    bias = _na_bias_table(rpb * LOG2E, rows)
    case = lambda g: (g > 0).astype(jnp.int32) + (g == groups - 1).astype(jnp.int32)
    return pl.pallas_call(
        _na_kernel,
        grid=(batch, groups),
        in_specs=[pl.BlockSpec((nqry, NA_WIDTH), lambda b, g: (b * groups + g, 0)),
                  pl.BlockSpec((seq, NA_WIDTH), lambda b, g: (b, 0)),
                  pl.BlockSpec((seq, NA_WIDTH), lambda b, g: (b, 0)),
                  pl.BlockSpec((1, NA_HEADS, nqry, nkeys), lambda b, g: (case(g), 0, 0, 0))],
        out_specs=pl.BlockSpec((nqry, NA_WIDTH), lambda b, g: (b * groups + g, 0)),
        out_shape=jax.ShapeDtypeStruct((batch * seq, NA_WIDTH), BF16),
        scratch_shapes=[pltpu.VMEM((nqry, nkeys), F32), pltpu.VMEM((nqry, nkeys), F32)],
        compiler_params=_cparams(("parallel", "arbitrary")),
        name="na_attention",
    )(nq, nk, nv, bias)


def _gelu_tanh(x):
    return 0.5 * x * (1.0 + jnp.tanh(math.sqrt(2.0 / math.pi) * (x + 0.044715 * (x * x * x))))


def _mix_ffn_kernel(a_ref, ap_ref, an_ref, b_ref, bp_ref, bn_ref, x_ref, xp_ref, xn_ref, wa_ref, wb_ref,
                    gmix_ref, gpre_ref, wg_ref, wu_ref, cw_ref, wd_ref, gpost_ref, o_ref, act_ref, *, n_seq):
    i = pl.program_id(0)
    tm = x_ref.shape[0]
    ext = tm + 2 * HALO
    with_halo = lambda prev, mid, nxt: jnp.concatenate([prev[...], mid[...], nxt[...]], axis=0)
    mix = (_dot(with_halo(ap_ref, a_ref, an_ref), wa_ref[...])
           + _dot(with_halo(bp_ref, b_ref, bn_ref), wb_ref[...]))
    x1 = with_halo(xp_ref, x_ref, xn_ref) + _rms(mix, gmix_ref[...])
    hn = _rms(x1, gpre_ref[...])
    r = lax.broadcasted_iota(jnp.int32, (ext, 1), 0)
    outside = ((r < HALO) & (i % n_seq == 0)) | ((r >= HALO + tm) & (i % n_seq == n_seq - 1))
    h_ext = jnp.where(outside, 0.0, hn).astype(BF16)
    hb = h_ext[HALO:HALO + tm]
    for j in range(D_FF // FFN_CHUNK):
        cs = slice(j * FFN_CHUNK, (j + 1) * FFN_CHUNK)
        gate = _dot(h_ext, wg_ref[:, cs])
        up = _dot(hb, wu_ref[:, cs])
        g_prev = pltpu.roll(gate, 1, 0)[HALO:HALO + tm]
        g_next = pltpu.roll(gate, ext - 1, 0)[HALO:HALO + tm]
        conv = g_prev * cw_ref[0:1, cs] + gate[HALO:HALO + tm] * cw_ref[1:2, cs] + g_next * cw_ref[2:3, cs]
        act_ref[:, cs] = (_gelu_tanh(conv) * up).astype(BF16)
    f = _dot(act_ref[...], wd_ref[...])
    o_ref[...] = x1[HALO:HALO + tm] + _rms(f, gpost_ref[...])


def _mix_ffn(a, b, w_out, x2, gmix, gpre, w_gate, w_up, conv_w, w_down, gpost, seq):
    t_rows = x2.shape[0]
    tm = ROW_TILE
    n_seq = seq // tm
    hb = tm // HALO
    n_halo = t_rows // HALO
    half = w_out.shape[0] // 2
    row = lambda n: pl.BlockSpec((tm, n), lambda i: (i, 0))
    prev = lambda n: pl.BlockSpec((HALO, n), lambda i: (jnp.maximum(i * hb - 1, 0), 0))
    nxt = lambda n: pl.BlockSpec((HALO, n), lambda i: (jnp.minimum((i + 1) * hb, n_halo - 1), 0))
    tile = lambda n: [row(n), prev(n), nxt(n)]
    return pl.pallas_call(
        functools.partial(_mix_ffn_kernel, n_seq=n_seq),
        grid=(t_rows // tm,),
        in_specs=tile(half) + tile(half) + tile(D_MODEL) + [
            _const_spec((half, D_MODEL)), _const_spec((half, D_MODEL)), _const_spec((1, D_MODEL)),
            _const_spec((1, D_MODEL)), _const_spec((D_MODEL, D_FF)), _const_spec((D_MODEL, D_FF)),
            _const_spec((3, D_FF)), _const_spec((D_FF, D_MODEL)), _const_spec((1, D_MODEL))],
        out_specs=row(D_MODEL),
        out_shape=jax.ShapeDtypeStruct((t_rows, D_MODEL), F32),
        scratch_shapes=[pltpu.VMEM((tm, D_FF), BF16)],
        compiler_params=_cparams(("parallel",)),
        name="mix_ffn",
    )(a, a, a, b, b, b, x2, x2, x2, w_out[:half].astype(BF16), w_out[half:].astype(BF16), gmix[None, :],
      gpre[None, :], w_gate.astype(BF16), w_up.astype(BF16), conv_w, w_down.astype(BF16), gpost[None, :])


def _front1_kernel(x_ref, g_ref, w_ref, tab_ref, rq_ref, rkt_ref, rv_ref, rg_ref, hy_ref):
    hn = _rms(x_ref[...], g_ref[...]).astype(BF16)
    r = _dot(hn, w_ref[:, :4 * RET_W])
    cosr = tab_ref[:, 0:128]
    sinr = tab_ref[:, 128:256]
    kscale = RET_DIM ** -0.5
    c = RET_CHUNK
    for h in range(RET_HEADS):
        qs = slice(h * RET_DIM, (h + 1) * RET_DIM)
        ks = slice(RET_W + h * RET_DIM, RET_W + (h + 1) * RET_DIM)
        qh = r[:, qs]
        kh = r[:, ks]
        rq_ref[:, qs] = (qh * cosr + pltpu.roll(qh, RET_DIM // 2, 1) * sinr).astype(BF16)
        krot = (kh * cosr + pltpu.roll(kh, RET_DIM // 2, 1) * sinr) * kscale
        for j in range(x_ref.shape[0] // c):
            rkt_ref[h, j] = krot[j * c:(j + 1) * c].T.astype(BF16)
    rv_ref[...] = r[:, 2 * RET_W:3 * RET_W].astype(BF16)
    rg_ref[...] = r[:, 3 * RET_W:]
    hy_ref[...] = _dot(hn, w_ref[:, 4 * RET_W:]).astype(BF16)


def _front1(x2, g, c_w_in, seq):
    t_rows = x2.shape[0]
    tm = ROW_TILE
    n_seq = seq // tm
    cos, sin = _rope_tables(seq, RET_DIM)
    tab = jnp.concatenate([cos, cos, -sin, sin], axis=1)
    w = c_w_in.astype(BF16)
    row = lambda n: pl.BlockSpec((tm, n), lambda i: (i, 0))
    kt_shape = (RET_HEADS, t_rows // RET_CHUNK, RET_DIM, RET_CHUNK)
    kt_spec = pl.BlockSpec((RET_HEADS, tm // RET_CHUNK, RET_DIM, RET_CHUNK), lambda i: (0, i, 0, 0))
    out_shapes = [jax.ShapeDtypeStruct((t_rows, RET_W), BF16), jax.ShapeDtypeStruct(kt_shape, BF16),
                  jax.ShapeDtypeStruct((t_rows, RET_W), BF16), jax.ShapeDtypeStruct((t_rows, RET_W), F32),
                  jax.ShapeDtypeStruct((t_rows, 3 * HY_WIDTH), BF16)]
    return pl.pallas_call(
        _front1_kernel,
        grid=(t_rows // tm,),
        in_specs=[row(D_MODEL), _const_spec((1, D_MODEL)), _const_spec(w.shape),
                  pl.BlockSpec((tm, 256), lambda i: (i % n_seq, 0))],
        out_specs=[row(RET_W), kt_spec, row(RET_W), row(RET_W), row(3 * HY_WIDTH)],
        out_shape=out_shapes,
        compiler_params=_cparams(("parallel",)),
        name="front1",
    )(x2, g[None, :], w, tab)


def _log_sigmoid(x):
    return jnp.minimum(x, 0.0) - jnp.log1p(jnp.exp(-jnp.abs(x)))


def _ret_kernel(q_ref, kt_ref, v_ref, g_ref, dec_ref, o_ref, intra_ref, kv_ref, st_ref):
    seq = q_ref.shape[0]
    c = RET_CHUNK
    n_chunks = seq // c
    lf = _log_sigmoid(dec_ref[0, 0:1, :])
    lb = _log_sigmoid(dec_ref[0, 1:2, :])
    t = lax.broadcasted_iota(jnp.int32, (c, c), 0).astype(F32)
    s_idx = lax.broadcasted_iota(jnp.int32, (c, c), 1).astype(F32)
    diff = t - s_idx
    dmat = jnp.where(diff >= 0, jnp.exp(jnp.maximum(diff, 0.0) * lf),
                     jnp.exp(jnp.maximum(-diff, 0.0) * lb))
    xi_f = jnp.exp((t + 1.0) * lf)
    xi_b = jnp.exp((c - t) * lb)
    zeta_f = jnp.exp((c - 1.0 - s_idx) * lf)
    zeta_b = jnp.exp(s_idx * lb)
    g_f = jnp.exp(c * lf)
    g_b = jnp.exp(c * lb)

    def chunk_a(n, carry):
        rs = pl.ds(pl.multiple_of(n * c, c), c)
        qn = q_ref[rs, :]
        ktn = kt_ref[0, n]
        vn = v_ref[rs, :]
        sc = (_dot(qn, ktn) * dmat).astype(BF16)
        intra_ref[rs, :] = _dot(sc, vn)
        ktf = ktn.astype(F32)
        kz = jnp.concatenate([ktf * zeta_f, ktf * zeta_b], axis=0).astype(BF16)
        kv_ref[n] = _dot(kz, vn)
        return carry

    lax.fori_loop(0, n_chunks, chunk_a, 0, unroll=RET_UNROLL)

    def scan_f(n, st):
        st_ref[n, :, :c] = st.astype(BF16)
        return g_f * st + kv_ref[n, :c, :]

    lax.fori_loop(0, n_chunks, scan_f, jnp.zeros((c, c), F32))

    def scan_b(m, st):
        n = n_chunks - 1 - m
        st_ref[n, :, c:] = st.astype(BF16)
        return g_b * st + kv_ref[n, c:, :]

    lax.fori_loop(0, n_chunks, scan_b, jnp.zeros((c, c), F32))

    def chunk_c(n, carry):
        rs = pl.ds(pl.multiple_of(n * c, c), c)
        cross = _dot(q_ref[rs, :], st_ref[n])
        o = intra_ref[rs, :] + cross[:, :c] * xi_f + cross[:, c:] * xi_b
        o = o * lax.rsqrt(jnp.mean(o * o, axis=-1, keepdims=True) + EPS)
        gate = g_ref[rs, :]
        o_ref[rs, :] = (o * (gate * jax.nn.sigmoid(gate))).astype(BF16)
        return carry

    lax.fori_loop(0, n_chunks, chunk_c, 0, unroll=RET_UNROLL)


def _retention(rq, rkt, rv, rg, decay_fwd, decay_bwd, batch, seq):
    dec = jnp.stack([decay_fwd, decay_bwd], axis=1)
    dec = jnp.broadcast_to(dec[:, :, None], (RET_HEADS, 2, LANES)).astype(F32)
    n_chunks = seq // RET_CHUNK
    blk = pl.BlockSpec((seq, RET_DIM), lambda b, h: (b, h))
    return pl.pallas_call(
        _ret_kernel,
        grid=(batch, RET_HEADS),
        in_specs=[blk, pl.BlockSpec((1, n_chunks, RET_DIM, RET_CHUNK), lambda b, h: (h, b, 0, 0)),
                  blk, blk, pl.BlockSpec((1, 2, LANES), lambda b, h: (h, 0, 0))],
        out_specs=blk,
        out_shape=jax.ShapeDtypeStruct((batch * seq, RET_W), BF16),
        scratch_shapes=[pltpu.VMEM((seq, RET_DIM), F32),
                        pltpu.VMEM((n_chunks, 2 * RET_DIM, RET_DIM), F32),
                        pltpu.VMEM((n_chunks, RET_DIM, 2 * RET_DIM), BF16)],
        compiler_params=_cparams(("parallel", "parallel")),
        name="retention",
    )(rq, rkt, rv, rg, dec)


def _fft_tables():
    n = FFT_N1 * FFT_N2
    n2 = np.arange(FFT_N2)[:, None, None]
    k1 = np.arange(FFT_N1)[None, :, None]
    n1 = np.arange(FFT_N1 // 2)[None, None, :]
    ang = 2.0 * np.pi * ((k1 * (FFT_N2 * n1 + n2)) % n) / n
    c1, s1 = np.cos(ang), np.sin(ang)
    f1 = np.concatenate([c1, -s1], axis=1)
    f1c = np.concatenate([np.concatenate([c1, s1], axis=2),
                          np.concatenate([-s1, c1], axis=2)], axis=1)
    c3, s3 = c1.transpose(0, 2, 1), s1.transpose(0, 2, 1)
    g3c = np.concatenate([np.concatenate([c3, -s3], axis=2),
                          np.concatenate([s3, c3], axis=2)], axis=1)
    a = np.arange(FFT_N2)
    ang2 = 2.0 * np.pi * ((a[:, None] * a[None, :]) % FFT_N2) / FFT_N2
    cc, ss = np.cos(ang2), np.sin(ang2)
    g2 = np.block([[cc, ss], [-ss, cc]])
    g2i = np.block([[cc, -ss], [ss, cc]])
    return dict(f1=f1, f1c=f1c, g2=g2, g2i=g2i, g3c=g3c)


def _split_bf16(a):
    hi = jnp.asarray(a, F32).astype(BF16)
    lo = (jnp.asarray(a, F32) - hi.astype(F32)).astype(BF16)
    return hi, lo


def _dot3(a_hi, a_lo, x):
    x_hi = x.astype(BF16)
    x_lo = (x - x_hi.astype(F32)).astype(BF16)
    return _dot(a_hi, x_hi) + _dot(a_hi, x_lo) + _dot(a_lo, x_hi)


def _hymlp_kernel(z_ref, w1_ref, b1_ref, w2_ref, b2_ref, w3_ref, b3_ref, fr_ref, h_ref):
    hp = lax.Precision.HIGHEST
    fr = fr_ref[...]

    def dense(h, w_ref, b_ref):
        return jnp.sin(fr * (jnp.dot(h, w_ref[...], precision=hp, preferred_element_type=F32) + b_ref[...]))

    h_ref[...] = dense(dense(dense(z_ref[...], w1_ref, b1_ref), w2_ref, b2_ref), w3_ref, b3_ref)


def _hyfilt_kernel(h_ref, w4f_ref, w4b_ref, dl_ref, f1h_ref, f1l_ref, g2h_ref, g2l_ref, kf_ref,
                   t_ref, a_ref):
    seq = h_ref.shape[0]
    hp = lax.Precision.HIGHEST
    blk = FFT_N2
    rows = HY_MLP_ROWS
    inv_n = 1.0 / (FFT_N1 * FFT_N2)
    dl = jnp.abs(dl_ref[...])

    for w4_ref, backward in ((w4f_ref, False), (w4b_ref, True)):

        def fill(i, carry):
            row0 = pl.multiple_of(i * rows, rows)
            t_idx = row0 + lax.broadcasted_iota(jnp.int32, (rows, LANES), 0)
            window = jnp.exp(-(t_idx.astype(F32) / (seq - 1)) * dl)
            filt = jnp.dot(h_ref[pl.ds(row0, rows), :], w4_ref[...], precision=hp,
                           preferred_element_type=F32) * window
            if backward:
                filt = jnp.where(t_idx == 0, 0.0, filt)
            for q in range(rows // blk):
                dst = pl.multiple_of((i * (rows // blk) + q) * PITCH_T, 8)
                t_ref[pl.ds(dst, blk), :] = filt[q * blk:(q + 1) * blk]
            return carry

        lax.fori_loop(0, seq // rows, fill, 0)

        def stage1(n2, carry):
            xs = t_ref[pl.ds(n2, FFT_N1 // 2, stride=PITCH_T), :]
            r = _dot3(f1h_ref[n2], f1l_ref[n2], xs)
            a_ref[pl.ds(n2, FFT_N1, stride=PITCH_A), :] = r[:FFT_N1]
            a_ref[pl.ds(FFT_N2 + n2, FFT_N1, stride=PITCH_A), :] = r[FFT_N1:]
            return carry

        lax.fori_loop(0, FFT_N2, stage1, 0, unroll=FFT_UNROLL)

        def stage2(k1, carry):
            rs = pl.ds(pl.multiple_of(k1 * PITCH_A, 8), 2 * FFT_N2)
            y = _dot3(g2h_ref[...], g2l_ref[...], a_ref[rs, :]) * inv_n
            if backward:
                kf_ref[k1] = kf_ref[k1] + jnp.concatenate([y[:FFT_N2], -y[FFT_N2:]], axis=0)
            else:
                kf_ref[k1] = y
            return carry

        lax.fori_loop(0, FFT_N1, stage2, 0, unroll=FFT_UNROLL)


def _hyena_filter_spectrum(w1, b1, w2, b2, w3, b3, w4, freq, seq, tables):
    f1, g2 = tables["f1"], tables["g2"]
    t = jnp.arange(seq, dtype=F32) / (seq - 1)
    bands = (HY_EMB_DIM - 1) // 2
    w = 2.0 * math.pi * jnp.arange(seq, dtype=F32) / seq
    f = jnp.linspace(1e-4, bands - 1, bands, dtype=F32)
    fw = f[None, :] * w[:, None]
    z = jnp.concatenate([t[:, None], jnp.cos(fw), -jnp.sin(fw)], axis=-1)
    z = jnp.pad(z, ((0, 0), (0, LANES - HY_EMB_DIM)))
    w1p = jnp.pad(w1, ((0, LANES - HY_EMB_DIM), (0, 0)))
    max_decay = math.log(HY_TARGET) / HY_FAST_DECAY
    min_decay = math.log(HY_TARGET) / HY_SLOW_DECAY
    deltas = jnp.linspace(min_decay, max_decay, HY_WIDTH, dtype=F32)[None, :]
    f1h, f1l = _split_bf16(f1)
    g2h, g2l = _split_bf16(g2)
    n_ct = HY_WIDTH // LANES
    hid = HY_FILTER_HIDDEN
    rows_t = (FFT_N1 // 2) * PITCH_T
    rows_a = FFT_N1 * PITCH_A
    h3 = pl.pallas_call(
        _hymlp_kernel,
        grid=(seq // HY_MLP_ROWS,),
        in_specs=[pl.BlockSpec((HY_MLP_ROWS, LANES), lambda i: (i, 0)), _const_spec((LANES, hid)),
                  _const_spec((1, hid)), _const_spec((hid, hid)), _const_spec((1, hid)),
                  _const_spec((hid, hid)), _const_spec((1, hid)), _const_spec((1, hid))],
        out_specs=pl.BlockSpec((HY_MLP_ROWS, hid), lambda i: (i, 0)),
        out_shape=jax.ShapeDtypeStruct((seq, hid), F32),
        compiler_params=_cparams(("parallel",)),
        name="hyena_mlp",
    )(z, w1p, b1[None, :], w2, b2[None, :], w3, b3[None, :], freq[None, :])
    return pl.pallas_call(
        _hyfilt_kernel,
        grid=(n_ct,),
        in_specs=[_const_spec((seq, hid)),
                  pl.BlockSpec((hid, LANES), lambda c: (0, c)),
                  pl.BlockSpec((hid, LANES), lambda c: (0, n_ct + c)),
                  pl.BlockSpec((1, LANES), lambda c: (0, c)),
                  _const_spec(f1h.shape), _const_spec(f1l.shape), _const_spec(g2h.shape),
                  _const_spec(g2l.shape)],
        out_specs=pl.BlockSpec((FFT_N1, 2 * FFT_N2, LANES), lambda c: (0, 0, c)),
        out_shape=jax.ShapeDtypeStruct((FFT_N1, 2 * FFT_N2, HY_WIDTH), F32),
        scratch_shapes=[pltpu.VMEM((rows_t, LANES), F32), pltpu.VMEM((rows_a, LANES), F32)],
        compiler_params=_cparams(("arbitrary",)),
        name="hyena_filter",
    )(h3, w4, w4, deltas, f1h, f1l, g2h, g2l)


def _hyconv_kernel(x0_ref, x1_ref, v_ref, c0_ref, c1_ref, cv_ref, bias_ref, kf_ref, f1_ref, g2_ref,
                   g2i_ref, g3_ref, o_ref, ua_ref, ub_ref, a_ref, b_ref):
    seq = x0_ref.shape[1]
    blk = FFT_N2
    half = FFT_N1 // 2
    pad = BF16_SUBLANES
    batches = tuple((x0_ref.at[s], x1_ref.at[s], v_ref.at[s], u_ref, o_ref.at[s])
                    for s, u_ref in enumerate((ua_ref, ub_ref)))

    def dwconv(ref, w_ref, start, nrows, off, zero_first, zero_last):
        win = ref[pl.ds(start, nrows), :].astype(F32)
        idx = lax.broadcasted_iota(jnp.int32, (nrows, LANES), 0)
        prev = pltpu.roll(win, 1, 0)
        nxt = pltpu.roll(win, nrows - 1, 0)
        if zero_first:
            prev = jnp.where(idx == 0, 0.0, prev)
        if zero_last:
            nxt = jnp.where(idx == nrows - 1, 0.0, nxt)
        out = prev * w_ref[0:1, :] + win * w_ref[1:2, :] + nxt * w_ref[2:3, :]
        return out[off:off + blk]

    def fill(n1, start, nrows, off, zero_first=False, zero_last=False):
        args = (start, nrows, off, zero_first, zero_last)
        for x0s, x1s, vs, u_ref, os in batches:
            u = dwconv(vs, cv_ref, *args) * dwconv(x1s, c1_ref, *args)
            u_ref[pl.ds(_aligned(n1 * PITCH_T, 8), blk), :] = u
            os[pl.ds(_aligned(n1 * blk, blk), blk), :] = dwconv(x0s, c0_ref, *args).astype(BF16)

    fill(0, 0, blk + pad, 0, zero_first=True)

    def fill_mid(n1, carry):
        fill(n1, _aligned(n1 * blk - pad, pad), blk + 2 * pad, pad)
        return carry

    lax.fori_loop(1, half - 1, fill_mid, 0, unroll=2)
    fill(half - 1, seq - blk - pad, blk + pad, pad, zero_last=True)

    def stage1(n2, carry):
        xs = jnp.concatenate([ua_ref[pl.ds(n2, half, stride=PITCH_T), :],
                              ub_ref[pl.ds(n2, half, stride=PITCH_T), :]], axis=0).astype(BF16)
        r = _dot(f1_ref[n2], xs)
        a_ref[pl.ds(n2, FFT_N1, stride=PITCH_A), :] = r[:FFT_N1]
        a_ref[pl.ds(FFT_N2 + n2, FFT_N1, stride=PITCH_A), :] = r[FFT_N1:]
        return carry

    lax.fori_loop(0, FFT_N2, stage1, 0, unroll=FFT_UNROLL)

    def group_blocks(j):
        k1s = [j * FFT_GROUP + g for g in range(FFT_GROUP)]
        return k1s, [pl.ds(pl.multiple_of(k1 * PITCH_A, 8), 2 * FFT_N2) for k1 in k1s]

    def stage2_fwd(j, carry):
        k1s, blocks = group_blocks(j)
        w = jnp.concatenate([a_ref[rs, :].astype(BF16) for rs in blocks], axis=1)
        y = _dot(g2_ref[...], w)
        kf = jnp.concatenate([kf_ref[k1] for k1 in k1s], axis=1)
        yr, yi = y[:FFT_N2], y[FFT_N2:]
        kr, ki = kf[:FFT_N2], kf[FFT_N2:]
        z = jnp.concatenate([yr * kr - yi * ki, yr * ki + yi * kr], axis=0)
        for g, rs in enumerate(blocks):
            a_ref[rs, :] = z[:, g * LANES:(g + 1) * LANES]
        return carry

    lax.fori_loop(0, FFT_N1 // FFT_GROUP, stage2_fwd, 0, unroll=2)

    def stage2_inv(j, carry):
        k1s, blocks = group_blocks(j)
        z = jnp.concatenate([a_ref[rs, :].astype(BF16) for rs in blocks], axis=1)
        bv = _dot(g2i_ref[...], z)
        for g, k1 in enumerate(k1s):
            ls = slice(g * LANES, (g + 1) * LANES)
            b_ref[pl.ds(k1, FFT_N2, stride=PITCH_B), :] = bv[:FFT_N2, ls]
            b_ref[pl.ds(FFT_N1 + k1, FFT_N2, stride=PITCH_B), :] = bv[FFT_N2:, ls]
        return carry

    lax.fori_loop(0, FFT_N1 // FFT_GROUP, stage2_inv, 0, unroll=2)
    bias = bias_ref[...]

    def stage3(n2, carry):
        rs = pl.ds(pl.multiple_of(n2 * PITCH_B, 8), 2 * FFT_N1)
        y = _dot(g3_ref[n2], b_ref[rs, :].astype(BF16))
        ts = pl.ds(n2, half, stride=PITCH_T)
        ua_ref[ts, :] = y[:half] + ua_ref[ts, :] * bias
        ub_ref[ts, :] = y[half:] + ub_ref[ts, :] * bias
        return carry

    lax.fori_loop(0, FFT_N2, stage3, 0, unroll=FFT_UNROLL)

    def finish(n1, carry):
        src = pl.ds(pl.multiple_of(n1 * PITCH_T, 8), blk)
        dst = pl.ds(pl.multiple_of(n1 * blk, blk), blk)
        for _, _, _, u_ref, os in batches:
            os[dst, :] = (u_ref[src, :] * os[dst, :].astype(F32)).astype(BF16)
        return carry

    lax.fori_loop(0, half, finish, 0, unroll=FFT_UNROLL)


def _hyena_conv(hy, short_conv, hy_bias, kf, batch, seq, tables):
    n_ct = HY_WIDTH // LANES
    rows_t = (FFT_N1 // 2) * PITCH_T
    col = lambda off: pl.BlockSpec((2, seq, LANES), lambda c, b: (b, 0, off + c))
    cw = lambda off: pl.BlockSpec((3, LANES), lambda c, b: (0, off + c))
    const = lambda name: jnp.asarray(tables[name], F32).astype(BF16)
    f1c, g2, g2i, g3c = const("f1c"), const("g2"), const("g2i"), const("g3c")
    hy3 = hy.reshape(batch, seq, 3 * HY_WIDTH)
    y = pl.pallas_call(
        _hyconv_kernel,
        grid=(n_ct, batch // 2),
        in_specs=[col(0), col(n_ct), col(2 * n_ct), cw(0), cw(n_ct), cw(2 * n_ct),
                  pl.BlockSpec((1, LANES), lambda c, b: (0, c)),
                  pl.BlockSpec((FFT_N1, 2 * FFT_N2, LANES), lambda c, b: (0, 0, c),
                               pipeline_mode=pl.Buffered(1)),
                  _const_spec(f1c.shape), _const_spec(g2.shape), _const_spec(g2i.shape),
                  _const_spec(g3c.shape)],
        out_specs=col(0),
        out_shape=jax.ShapeDtypeStruct((batch, seq, HY_WIDTH), BF16),
        scratch_shapes=[pltpu.VMEM((rows_t, LANES), F32), pltpu.VMEM((rows_t, LANES), F32),
                        pltpu.VMEM((FFT_N1 * PITCH_A, LANES), F32),
                        pltpu.VMEM((FFT_N2 * PITCH_B, LANES), F32)],
        compiler_params=_cparams(("arbitrary", "arbitrary"), HYCONV_VMEM_LIMIT),
        name="hyena_conv",
    )(hy3, hy3, hy3, short_conv, short_conv, short_conv, hy_bias[None, :], kf, f1c, g2, g2i, g3c)
    return y.reshape(batch * seq, HY_WIDTH)


def kernel(x, mix_pre_norm, mix_post_norm, ffn_pre_norm, ffn_post_norm, ffn_w_gate, ffn_w_up, ffn_conv, ffn_w_down, a_w_in, a_q_norm, a_w_q_up, a_kv_norm, a_w_kv_up, a_rpb, a_w_out, c_w_in, c_decay_fwd, c_decay_bwd, c_short_conv, c_filt_w1, c_filt_b1, c_filt_w2, c_filt_b2, c_filt_w3, c_filt_b3, c_filt_w4, c_filt_freq, c_hy_bias, c_w_out):
    batch, seq, d = x.shape
    assert d == D_MODEL and seq * 2 == FFT_N1 * FFT_N2 and seq % ROW_TILE == 0 and batch % 2 == 0
    x2 = x.reshape(batch * seq, d)

    q, k, v2, nq, nk, nv = _front0(x2, mix_pre_norm[0], a_w_in[0], a_q_norm[0], a_w_q_up[0],
                                   a_kv_norm[0], a_w_kv_up[0], seq)
    a = _mla_attention(q, k, v2, batch, seq)
    b = _na_attention(nq, nk, nv, a_rpb[0], batch, seq)
    x2 = _mix_ffn(a, b, a_w_out[0], x2, mix_post_norm[0], ffn_pre_norm[0], ffn_w_gate[0], ffn_w_up[0],
                  ffn_conv[0], ffn_w_down[0], ffn_post_norm[0], seq)

    rq, rk, rv, rg, hy = _front1(x2, mix_pre_norm[1], c_w_in[0], seq)
    c = _retention(rq, rk, rv, rg, c_decay_fwd[0], c_decay_bwd[0], batch, seq)
    tables = _fft_tables()
    kf = _hyena_filter_spectrum(c_filt_w1[0], c_filt_b1[0], c_filt_w2[0], c_filt_b2[0], c_filt_w3[0],
                                c_filt_b3[0], c_filt_w4[0], c_filt_freq[0], seq, tables)
    dd = _hyena_conv(hy, c_short_conv[0], c_hy_bias[0], kf, batch, seq, tables)
    x2 = _mix_ffn(c, dd, c_w_out[0], x2, mix_post_norm[1], ffn_pre_norm[1], ffn_w_gate[1], ffn_w_up[1],
                  ffn_conv[1], ffn_w_down[1], ffn_post_norm[1], seq)
    return x2.reshape(batch, seq, d)
```

```python
import functools
import math

import numpy as np
import jax
import jax.numpy as jnp
from jax import lax
from jax.experimental import pallas as pl
from jax.experimental.pallas import tpu as pltpu

F32 = jnp.float32
BF16 = jnp.bfloat16

D_MODEL = 1024
GRID_W = 64
MLA_HEADS = 8
MLA_Q_LORA = 256
MLA_KV_LORA = 128
MLA_NOPE = 64
MLA_ROPE = 32
MLA_V = 64
ROPE_THETA = 10000.0
NA_HEADS = 8
NA_HEAD_DIM = 64
NA_WIN_H = 8
NA_WIN_W = 16
NA_WIDTH = NA_HEADS * NA_HEAD_DIM
RET_HEADS = 4
RET_DIM = 128
RET_CHUNK = 128
RET_W = RET_HEADS * RET_DIM
HY_WIDTH = 512
HY_EMB_DIM = 33
HY_FILTER_HIDDEN = 64
HY_FAST_DECAY = 0.3
HY_SLOW_DECAY = 1.5
HY_TARGET = 1e-2
D_FF = 2816
EPS = 1e-6
LOG2E = math.log2(math.e)

LANES = 128
BF16_SUBLANES = 16
VMEM_LIMIT = 56 * 1024 * 1024

ROW_TILE = 512
MLA_Q_TILE = 1024
MLA_SUB_TILE = 256
MLA_K_CHUNK = 512
NA_ROWS = 4
NA_KEY_ROWS = 12
NA_K_CHUNK = 256
FFN_CHUNK = 256
RET_UNROLL = 4
FFT_UNROLL = 4
FFT_GROUP = 4
HY_MLP_ROWS = 256
HALO = BF16_SUBLANES

FFT_N1 = 128
FFT_N2 = 64
PITCH_T = 72
PITCH_A = 136
PITCH_B = 264


def _cparams(sem, vmem_limit=VMEM_LIMIT):
    return pltpu.CompilerParams(dimension_semantics=sem, vmem_limit_bytes=vmem_limit)


def _rms(x, g):
    return x * lax.rsqrt(jnp.mean(x * x, axis=-1, keepdims=True) + EPS) * g


def _dot(a, b):
    return jnp.dot(a, b, preferred_element_type=F32)


def _dot_nt(a, b):
    return lax.dot_general(a, b, (((1,), (1,)), ((), ())), preferred_element_type=F32)


def _const_spec(shape):
    nd = len(shape)
    return pl.BlockSpec(shape, lambda *_: (0,) * nd, pipeline_mode=pl.Buffered(1))


def _halo_specs(tm, t_rows, width):
    per = tm // HALO
    last = t_rows // HALO - 1
    return [pl.BlockSpec((tm, width), lambda i: (i, 0)),
            pl.BlockSpec((HALO, width), lambda i: (jnp.maximum(i * per - 1, 0), 0)),
            pl.BlockSpec((HALO, width), lambda i: (jnp.minimum((i + 1) * per, last), 0))]


def _rope_tables(length, dim):
    inv = ROPE_THETA ** (-jnp.arange(0, dim, 2, dtype=F32) / dim)
    ang = jnp.arange(length, dtype=F32)[:, None] * inv[None, :]
    return jnp.cos(ang), jnp.sin(ang)


def _front0_kernel(x_ref, g_ref, wc_ref, wn_ref, qn_ref, kvn_ref, wq_ref, wqr_ref, wkv_ref,
                   wv_ref, vone_ref, tab_ref, q_ref, k_ref, v_ref, nq_ref, nk_ref, nv_ref):
    hn = _rms(x_ref[...], g_ref[...]).astype(BF16)
    c = _dot(hn, wc_ref[...])
    n = _dot(hn, wn_ref[...])
    nq_ref[...] = (n[:, :NA_WIDTH] * (NA_HEAD_DIM ** -0.5 * LOG2E)).astype(BF16)
    nk_ref[...] = n[:, NA_WIDTH:2 * NA_WIDTH].astype(BF16)
    nv_ref[...] = n[:, 2 * NA_WIDTH:].astype(BF16)
    cqn = _rms(c[:, :MLA_Q_LORA], qn_ref[...]).astype(BF16)
    ckvn = _rms(c[:, MLA_Q_LORA:MLA_Q_LORA + MLA_KV_LORA], kvn_ref[...]).astype(BF16)
    kr = c[:, 384:512]
    krr = c[:, 512:640]
    cosq = tab_ref[:, 0:128]
    sinq = tab_ref[:, 128:256]
    cosk = tab_ref[:, 256:384]
    sink = tab_ref[:, 384:512]
    k_rope = kr * cosk + krr * sink
    q = _dot(cqn, wq_ref[...])
    qr = _dot(cqn, wqr_ref[...])
    kn = _dot(ckvn, wkv_ref[...])
    for h in range(MLA_HEADS):
        sl = slice(h * LANES, (h + 1) * LANES)
        q_ref[:, sl] = (q[:, sl] * cosq + qr[:, sl] * sinq).astype(BF16)
        k_ref[:, sl] = (kn[:, sl] + k_rope).astype(BF16)
    v_ref[...] = (_dot(ckvn, wv_ref[...]) + vone_ref[...]).astype(BF16)


def _front0(x2, g, a_w_in, a_q_norm, a_w_q_up, a_kv_norm, a_w_kv_up, seq):
    t_rows = x2.shape[0]
    tm = ROW_TILE
    n_seq = seq // tm
    w = a_w_in
    wkr = w[:, 384:416]
    zeros = lambda n: jnp.zeros((D_MODEL, n), F32)
    wkr_full = jnp.concatenate([zeros(64), wkr, zeros(32)], axis=1)
    wkr_rot = jnp.concatenate([zeros(64), -wkr[:, 16:], wkr[:, :16], zeros(32)], axis=1)
    wc = jnp.concatenate([w[:, :384], wkr_full, wkr_rot], axis=1).astype(BF16)
    wn = w[:, 416:].astype(BF16)
    wq3 = a_w_q_up.reshape(MLA_Q_LORA, MLA_HEADS, MLA_NOPE + MLA_ROPE)
    nope, rope = wq3[..., :MLA_NOPE], wq3[..., MLA_NOPE:]
    pad32 = jnp.zeros((MLA_Q_LORA, MLA_HEADS, 32), F32)
    pad64 = jnp.zeros((MLA_Q_LORA, MLA_HEADS, 64), F32)
    wq = jnp.concatenate([nope, rope, pad32], axis=-1).reshape(MLA_Q_LORA, -1).astype(BF16)
    wqr = jnp.concatenate([pad64, -rope[..., 16:], rope[..., :16], pad32],
                          axis=-1).reshape(MLA_Q_LORA, -1).astype(BF16)
    wkv3 = a_w_kv_up.reshape(MLA_KV_LORA, MLA_HEADS, MLA_NOPE + MLA_V)
    knope, vup = wkv3[..., :MLA_NOPE], wkv3[..., MLA_NOPE:]
    kpad = jnp.zeros((MLA_KV_LORA, MLA_HEADS, 64), F32)
    wkv = jnp.concatenate([knope, kpad], axis=-1).reshape(MLA_KV_LORA, -1).astype(BF16)
    vup4 = vup.reshape(MLA_KV_LORA, MLA_HEADS // 2, 2, MLA_V)
    vpad = jnp.zeros((MLA_KV_LORA, MLA_HEADS // 2, 64), F32)
    wv = jnp.concatenate([vup4[:, :, 0], vpad, vpad, vup4[:, :, 1]], axis=-1)
    wv = wv.reshape(MLA_KV_LORA, -1).astype(BF16)
    pair_one = np.zeros((256,), np.float32)
    pair_one[64] = 1.0
    pair_one[128] = 1.0
    vone = jnp.asarray(np.tile(pair_one, MLA_HEADS // 2)[None, :])
    cos, sin = _rope_tables(seq, MLA_ROPE)
    sc = (MLA_NOPE + MLA_ROPE) ** -0.5 * LOG2E
    z16 = jnp.zeros((seq, 32), F32)
    z64 = jnp.zeros((seq, 64), F32)
    tab = jnp.concatenate([
        jnp.full((seq, 64), sc, F32), cos * sc, cos * sc, z16,
        z64, sin * sc, sin * sc, z16,
        z64, cos, cos, z16,
        z64, sin, sin, z16], axis=1)
    out_shapes = [jax.ShapeDtypeStruct((t_rows, 1024), BF16)] * 3 + \
                 [jax.ShapeDtypeStruct((t_rows, NA_WIDTH), BF16)] * 3
    row = lambda n: pl.BlockSpec((tm, n), lambda i: (i, 0))
    return pl.pallas_call(
        _front0_kernel,
        grid=(t_rows // tm,),
        in_specs=[row(D_MODEL), _const_spec((1, D_MODEL)), _const_spec(wc.shape), _const_spec(wn.shape),
                  _const_spec((1, MLA_Q_LORA)), _const_spec((1, MLA_KV_LORA)), _const_spec(wq.shape),
                  _const_spec(wqr.shape), _const_spec(wkv.shape), _const_spec(wv.shape),
                  _const_spec((1, 1024)),
                  pl.BlockSpec((tm, 512), lambda i: (i % n_seq, 0))],
        out_specs=[row(1024), row(1024), row(1024), row(NA_WIDTH), row(NA_WIDTH), row(NA_WIDTH)],
        out_shape=out_shapes,
        compiler_params=_cparams(("parallel",)),
        name="front0",
    )(x2, g[None, :], wc, wn, a_q_norm[None, :], a_kv_norm[None, :], wq, wqr, wkv, wv, vone, tab)


def _pair_normalise(acc_even, acc_odd):
    lane = lax.broadcasted_iota(jnp.int32, acc_even.shape, 1)
    return jnp.where(lane < 64, acc_even / acc_even[:, 64:65], acc_odd / acc_odd[:, 0:1])


def _staggered(units, bufs, scores, weighted_values):
    accs = {}
    pending = None
    for i, unit in enumerate(units):
        s_ref = bufs[i % len(bufs)]
        mrow = scores(unit, s_ref)
        if pending is not None:
            accs[pending[0]] = weighted_values(*pending)
        pending = (unit, s_ref, mrow)
    accs[pending[0]] = weighted_values(*pending)
    return accs


def _mla_kernel(q_ref, k_ref, v_ref, o_ref, s0_ref, s1_ref):
    sub = MLA_SUB_TILE
    seq = k_ref.shape[0]
    n_sub = q_ref.shape[0] // sub
    chunks = [slice(c * MLA_K_CHUNK, (c + 1) * MLA_K_CHUNK) for c in range(seq // MLA_K_CHUNK)]
    head = lambda hh: slice(hh * LANES, (hh + 1) * LANES)

    def scores(unit, s_ref):
        r, hh = unit
        qh = q_ref[r * sub:(r + 1) * sub, head(hh)]
        m = jnp.full((sub, LANES), -jnp.inf, F32)
        for cs in chunks:
            s = _dot_nt(qh, k_ref[cs, head(hh)])
            s_ref[:, cs] = s
            for j in range(MLA_K_CHUNK // LANES):
                m = jnp.maximum(m, s[:, j * LANES:(j + 1) * LANES])
        return jnp.max(m, axis=-1, keepdims=True)

    def weighted_values(unit, s_ref, mrow):
        _, hh = unit
        acc = jnp.zeros((sub, LANES), F32)
        for cs in chunks:
            p = jnp.exp2(s_ref[:, cs] - mrow).astype(BF16)
            acc = acc + _dot(p, v_ref[cs, head(hh)])
        return acc

    units = [(r, hh) for r in range(n_sub) for hh in range(2)]
    accs = _staggered(units, (s0_ref, s1_ref), scores, weighted_values)
    for r in range(n_sub):
        o_ref[r * sub:(r + 1) * sub, :] = _pair_normalise(accs[(r, 0)], accs[(r, 1)]).astype(BF16)


def _mla_attention(q, k, v2, batch, seq):
    tq = MLA_Q_TILE
    nq = seq // tq
    return pl.pallas_call(
        _mla_kernel,
        grid=(batch, MLA_HEADS // 2, nq),
        in_specs=[pl.BlockSpec((tq, 256), lambda b, p, i: (b * nq + i, p)),
                  pl.BlockSpec((seq, 256), lambda b, p, i: (b, p)),
                  pl.BlockSpec((seq, 256), lambda b, p, i: (b, p))],
        out_specs=pl.BlockSpec((tq, LANES), lambda b, p, i: (b * nq + i, p)),
        out_shape=jax.ShapeDtypeStruct((batch * seq, MLA_HEADS * MLA_V), BF16),
        scratch_shapes=[pltpu.VMEM((MLA_SUB_TILE, seq), F32), pltpu.VMEM((MLA_SUB_TILE, seq), F32)],
        compiler_params=_cparams(("parallel", "parallel", "arbitrary")),
        name="mla_attention",
    )(q, k, v2)


def _na_bias_table(rpb, rows):
    kh, kw = NA_WIN_H, NA_WIN_W
    neg = -1e30
    c = np.arange(GRID_W)[:, None]
    kc = np.arange(GRID_W)[None, :]
    cst = np.clip(c - kw // 2, 0, GRID_W - kw)
    col_valid = (kc >= cst) & (kc < cst + kw)
    col_off = kc - c + (kw - 1)
    onehot = ((col_off[None] == np.arange(2 * kw - 1)[:, None, None]) & col_valid[None]).astype(np.float32)
    blocks = jnp.einsum('hdj,jck->hdck', rpb.astype(F32), jnp.asarray(onehot),
                        precision=lax.Precision.HIGHEST)
    blocks = jnp.where(col_valid, blocks, neg)
    masked = jnp.full((NA_HEADS, GRID_W, GRID_W), neg, F32)
    tables = []
    for r0, ws in ((0, 0), (2 * NA_ROWS, 2 * NA_ROWS - kh // 2), (rows - NA_ROWS, rows - NA_KEY_ROWS)):
        q_rows = []
        for ri in range(NA_ROWS):
            r = r0 + ri
            rs = min(max(r - kh // 2, 0), rows - kh)
            k_blocks = []
            for i in range(NA_KEY_ROWS):
                kr = ws + i
                k_blocks.append(blocks[:, kr - r + (kh - 1)] if rs <= kr < rs + kh else masked)
            q_rows.append(jnp.concatenate(k_blocks, axis=-1))
        tables.append(jnp.concatenate(q_rows, axis=1))
    return jnp.stack(tables)


def _na_kernel(q_ref, k_ref, v_ref, b_ref, o_ref, s0_ref, s1_ref):
    g = pl.program_id(1)
    rows = k_ref.shape[0] // GRID_W
    ws = jnp.clip(g * NA_ROWS - NA_WIN_H // 2, 0, rows - NA_KEY_ROWS)
    start = pl.multiple_of(ws * GRID_W, GRID_W)
    nkeys = NA_KEY_ROWS * GRID_W
    nqry = NA_ROWS * GRID_W
    lane = lax.broadcasted_iota(jnp.int32, (1, LANES), 1)
    pair = lambda p: slice(p * LANES, (p + 1) * LANES)
    chunks = [slice(c * NA_K_CHUNK, (c + 1) * NA_K_CHUNK) for c in range(nkeys // NA_K_CHUNK)]

    def scores(unit, s_ref):
        p, hh = unit
        q2 = q_ref[:, pair(p)]
        own = (lane < 64) if hh == 0 else (lane >= 64)
        qm = jnp.where(own, q2, jnp.zeros_like(q2))
        m = jnp.full((nqry, LANES), -jnp.inf, F32)
        for cs in chunks:
            k2 = k_ref[pl.ds(pl.multiple_of(start + cs.start, GRID_W), NA_K_CHUNK), pair(p)]
            s = _dot_nt(qm, k2) + b_ref[0, 2 * p + hh, :, cs]
            s_ref[:, cs] = s
            for j in range(NA_K_CHUNK // LANES):
                m = jnp.maximum(m, s[:, j * LANES:(j + 1) * LANES])
        return jnp.max(m, axis=-1, keepdims=True)

    def weighted_values(unit, s_ref, mrow):
        p, hh = unit
        own = (lane < 64) if hh == 0 else (lane >= 64)
        ones_lane = 64 if hh == 0 else 0
        acc = jnp.zeros((nqry, LANES), F32)
        for cs in chunks:
            v2 = v_ref[pl.ds(pl.multiple_of(start + cs.start, GRID_W), NA_K_CHUNK), pair(p)]
            vm = jnp.where(own, v2, jnp.where(lane == ones_lane, 1.0, 0.0).astype(BF16))
            acc = acc + _dot(jnp.exp2(s_ref[:, cs] - mrow).astype(BF16), vm)
        return acc

    units = [(p, hh) for p in range(NA_HEADS // 2) for hh in range(2)]
    accs = _staggered(units, (s0_ref, s1_ref), scores, weighted_values)
    for p in range(NA_HEADS // 2):
        o_ref[:, pair(p)] = _pair_normalise(accs[(p, 0)], accs[(p, 1)]).astype(BF16)


def _na_attention(nq, nk, nv, rpb, batch, seq):
    rows = seq // GRID_W
    groups = rows // NA_ROWS
    nqry = NA_ROWS * GRID_W
    nkeys = NA_KEY_ROWS * GRID_W
    bias = _na_bias_table(rpb * LOG2E, rows)
    case = lambda g: (g > 0).astype(jnp.int32) + (g == groups - 1).astype(jnp.int32)
    return pl.pallas_call(
        _na_kernel,
        grid=(batch, groups),
        in_specs=[pl.BlockSpec((nqry, NA_WIDTH), lambda b, g: (b * groups + g, 0)),
                  pl.BlockSpec((seq, NA_WIDTH), lambda b, g: (b, 0)),
                  pl.BlockSpec((seq, NA_WIDTH), lambda b, g: (b, 0)),
                  pl.BlockSpec((1, NA_HEADS, nqry, nkeys), lambda b, g: (case(g), 0, 0, 0))],
        out_specs=pl.BlockSpec((nqry, NA_WIDTH), lambda b, g: (b * groups + g, 0)),
        out_shape=jax.ShapeDtypeStruct((batch * seq, NA_WIDTH), BF16),
        scratch_shapes=[pltpu.VMEM((nqry, nkeys), F32), pltpu.VMEM((nqry, nkeys), F32)],
        compiler_params=_cparams(("parallel", "arbitrary")),
        name="na_attention",
    )(nq, nk, nv, bias)


def _gelu_tanh(x):
    return 0.5 * x * (1.0 + jnp.tanh(math.sqrt(2.0 / math.pi) * (x + 0.044715 * (x * x * x))))


def _mix_ffn_kernel(a_ref, ap_ref, an_ref, b_ref, bp_ref, bn_ref, x_ref, xp_ref, xn_ref, wa_ref, wb_ref,
                    gmix_ref, gpre_ref, wg_ref, wu_ref, cw_ref, wd_ref, gpost_ref, o_ref, act_ref, *, n_seq):
    i = pl.program_id(0)
    tm = x_ref.shape[0]
    ext = tm + 2 * HALO
    with_halo = lambda prev, mid, nxt: jnp.concatenate([prev[...], mid[...], nxt[...]], axis=0)
    mix = (_dot(with_halo(ap_ref, a_ref, an_ref), wa_ref[...])
           + _dot(with_halo(bp_ref, b_ref, bn_ref), wb_ref[...]))
    x1 = with_halo(xp_ref, x_ref, xn_ref) + _rms(mix, gmix_ref[...])
    hn = _rms(x1, gpre_ref[...])
    r = lax.broadcasted_iota(jnp.int32, (ext, 1), 0)
    outside = ((r < HALO) & (i % n_seq == 0)) | ((r >= HALO + tm) & (i % n_seq == n_seq - 1))
    h_ext = jnp.where(outside, 0.0, hn).astype(BF16)
    hb = h_ext[HALO:HALO + tm]
    for j in range(D_FF // FFN_CHUNK):
        cs = slice(j * FFN_CHUNK, (j + 1) * FFN_CHUNK)
        gate = _dot(h_ext, wg_ref[:, cs])
        up = _dot(hb, wu_ref[:, cs])
        g_prev = pltpu.roll(gate, 1, 0)[HALO:HALO + tm]
        g_next = pltpu.roll(gate, ext - 1, 0)[HALO:HALO + tm]
        conv = g_prev * cw_ref[0:1, cs] + gate[HALO:HALO + tm] * cw_ref[1:2, cs] + g_next * cw_ref[2:3, cs]
        act_ref[:, cs] = (_gelu_tanh(conv) * up).astype(BF16)
    f = _dot(act_ref[...], wd_ref[...])
    o_ref[...] = x1[HALO:HALO + tm] + _rms(f, gpost_ref[...])


def _mix_ffn(a, b, w_out, x2, gmix, gpre, w_gate, w_up, conv_w, w_down, gpost, seq):
    t_rows = x2.shape[0]
    tm = ROW_TILE
    n_seq = seq // tm
    half = w_out.shape[0] // 2
    tile = lambda n: _halo_specs(tm, t_rows, n)
    return pl.pallas_call(
        functools.partial(_mix_ffn_kernel, n_seq=n_seq),
        grid=(t_rows // tm,),
        in_specs=tile(half) + tile(half) + tile(D_MODEL) + [
            _const_spec((half, D_MODEL)), _const_spec((half, D_MODEL)), _const_spec((1, D_MODEL)),
            _const_spec((1, D_MODEL)), _const_spec((D_MODEL, D_FF)), _const_spec((D_MODEL, D_FF)),
            _const_spec((3, D_FF)), _const_spec((D_FF, D_MODEL)), _const_spec((1, D_MODEL))],
        out_specs=pl.BlockSpec((tm, D_MODEL), lambda i: (i, 0)),
        out_shape=jax.ShapeDtypeStruct((t_rows, D_MODEL), F32),
        scratch_shapes=[pltpu.VMEM((tm, D_FF), BF16)],
        compiler_params=_cparams(("parallel",)),
        name="mix_ffn",
    )(a, a, a, b, b, b, x2, x2, x2, w_out[:half].astype(BF16), w_out[half:].astype(BF16), gmix[None, :],
      gpre[None, :], w_gate.astype(BF16), w_up.astype(BF16), conv_w, w_down.astype(BF16), gpost[None, :])


def _front1_kernel(x_ref, xp_ref, xn_ref, g_ref, w_ref, sc_ref, tab_ref, rq_ref, rkt_ref, rv_ref,
                   rg_ref, u_ref, z0_ref, *, n_seq):
    i = pl.program_id(0)
    tm = x_ref.shape[0]
    ext = tm + 2 * HALO
    g = g_ref[...]
    hn = _rms(x_ref[...], g).astype(BF16)
    r = _dot(hn, w_ref[:, :4 * RET_W])
    cosr = tab_ref[:, 0:128]
    sinr = tab_ref[:, 128:256]
    kscale = RET_DIM ** -0.5
    c = RET_CHUNK
    for h in range(RET_HEADS):
        qs = slice(h * RET_DIM, (h + 1) * RET_DIM)
        ks = slice(RET_W + h * RET_DIM, RET_W + (h + 1) * RET_DIM)
        qh = r[:, qs]
        kh = r[:, ks]
        rq_ref[:, qs] = (qh * cosr + pltpu.roll(qh, RET_DIM // 2, 1) * sinr).astype(BF16)
        krot = (kh * cosr + pltpu.roll(kh, RET_DIM // 2, 1) * sinr) * kscale
        for j in range(tm // c):
            rkt_ref[h, j] = krot[j * c:(j + 1) * c].T.astype(BF16)
    rv_ref[...] = r[:, 2 * RET_W:3 * RET_W].astype(BF16)
    rg_ref[...] = r[:, 3 * RET_W:]
    keep_prev = jnp.where(i % n_seq == 0, 0.0, 1.0)
    keep_next = jnp.where(i % n_seq == n_seq - 1, 0.0, 1.0)
    h_ext = jnp.concatenate([(_rms(xp_ref[...], g) * keep_prev).astype(BF16), hn,
                             (_rms(xn_ref[...], g) * keep_next).astype(BF16)], axis=0)
    hy = _dot(h_ext, w_ref[:, 4 * RET_W:])
    z = (pltpu.roll(hy, 1, 0)[HALO:HALO + tm] * sc_ref[0:1, :] + hy[HALO:HALO + tm] * sc_ref[1:2, :]
         + pltpu.roll(hy, ext - 1, 0)[HALO:HALO + tm] * sc_ref[2:3, :])
    z0_ref[...] = z[:, :HY_WIDTH].astype(BF16)
    u_ref[...] = (z[:, 2 * HY_WIDTH:] * z[:, HY_WIDTH:2 * HY_WIDTH]).astype(BF16)


def _front1(x2, g, c_w_in, short_conv, seq):
    t_rows = x2.shape[0]
    tm = ROW_TILE
    n_seq = seq // tm
    cos, sin = _rope_tables(seq, RET_DIM)
    tab = jnp.concatenate([cos, cos, -sin, sin], axis=1)
    w = c_w_in.astype(BF16)
    row = lambda n: pl.BlockSpec((tm, n), lambda i: (i, 0))
    kt_shape = (RET_HEADS, t_rows // RET_CHUNK, RET_DIM, RET_CHUNK)
    kt_spec = pl.BlockSpec((RET_HEADS, tm // RET_CHUNK, RET_DIM, RET_CHUNK), lambda i: (0, i, 0, 0))
    out_shapes = [jax.ShapeDtypeStruct((t_rows, RET_W), BF16), jax.ShapeDtypeStruct(kt_shape, BF16),
                  jax.ShapeDtypeStruct((t_rows, RET_W), BF16), jax.ShapeDtypeStruct((t_rows, RET_W), F32),
                  jax.ShapeDtypeStruct((t_rows, HY_WIDTH), BF16), jax.ShapeDtypeStruct((t_rows, HY_WIDTH), BF16)]
    return pl.pallas_call(
        functools.partial(_front1_kernel, n_seq=n_seq),
        grid=(t_rows // tm,),
        in_specs=_halo_specs(tm, t_rows, D_MODEL) + [
            _const_spec((1, D_MODEL)), _const_spec(w.shape), _const_spec(short_conv.shape),
            pl.BlockSpec((tm, 256), lambda i: (i % n_seq, 0))],
        out_specs=[row(RET_W), kt_spec, row(RET_W), row(RET_W), row(HY_WIDTH), row(HY_WIDTH)],
        out_shape=out_shapes,
        compiler_params=_cparams(("parallel",)),
        name="front1",
    )(x2, x2, x2, g[None, :], w, short_conv, tab)


def _log_sigmoid(x):
    return jnp.minimum(x, 0.0) - jnp.log1p(jnp.exp(-jnp.abs(x)))


def _ret_kernel(q_ref, kt_ref, v_ref, g_ref, dec_ref, o_ref, sc_ref, kv_ref, st_ref):
    seq = q_ref.shape[0]
    c = RET_CHUNK
    n_chunks = seq // c
    lf = _log_sigmoid(dec_ref[0, 0:1, :])
    lb = _log_sigmoid(dec_ref[0, 1:2, :])
    t = lax.broadcasted_iota(jnp.int32, (c, c), 0).astype(F32)
    s_idx = lax.broadcasted_iota(jnp.int32, (c, c), 1).astype(F32)
    diff = t - s_idx
    dmat = jnp.where(diff >= 0, jnp.exp(jnp.maximum(diff, 0.0) * lf),
                     jnp.exp(jnp.maximum(-diff, 0.0) * lb))
    xi_f = jnp.exp((t + 1.0) * lf)
    xi_b = jnp.exp((c - t) * lb)
    zeta_f = jnp.exp((c - 1.0 - s_idx) * lf)
    zeta_b = jnp.exp(s_idx * lb)
    g_f = jnp.exp(c * lf)
    g_b = jnp.exp(c * lb)

    def chunk_a(n, carry):
        rs = pl.ds(pl.multiple_of(n * c, c), c)
        ktn = kt_ref[0, n]
        sc_ref[n] = (_dot(q_ref[rs, :], ktn) * dmat).astype(BF16)
        ktf = ktn.astype(F32)
        kz = jnp.concatenate([ktf * zeta_f, ktf * zeta_b], axis=0).astype(BF16)
        kv_ref[n] = _dot(kz, v_ref[rs, :])
        return carry

    lax.fori_loop(0, n_chunks, chunk_a, 0, unroll=RET_UNROLL)

    def scan_f(n, st):
        st_ref[n, :, :c] = st.astype(BF16)
        return g_f * st + kv_ref[n, :c, :]

    lax.fori_loop(0, n_chunks, scan_f, jnp.zeros((c, c), F32))

    def scan_b(m, st):
        n = n_chunks - 1 - m
        st_ref[n, :, c:] = st.astype(BF16)
        return g_b * st + kv_ref[n, c:, :]

    lax.fori_loop(0, n_chunks, scan_b, jnp.zeros((c, c), F32))

    def chunk_c(n, carry):
        rs = pl.ds(pl.multiple_of(n * c, c), c)
        cross = _dot(q_ref[rs, :], st_ref[n])
        o = _dot(sc_ref[n], v_ref[rs, :]) + cross[:, :c] * xi_f + cross[:, c:] * xi_b
        o = o * lax.rsqrt(jnp.mean(o * o, axis=-1, keepdims=True) + EPS)
        gate = g_ref[rs, :]
        o_ref[rs, :] = (o * (gate * jax.nn.sigmoid(gate))).astype(BF16)
        return carry

    lax.fori_loop(0, n_chunks, chunk_c, 0, unroll=RET_UNROLL)


def _retention(rq, rkt, rv, rg, decay_fwd, decay_bwd, batch, seq):
    dec = jnp.stack([decay_fwd, decay_bwd], axis=1)
    dec = jnp.broadcast_to(dec[:, :, None], (RET_HEADS, 2, LANES)).astype(F32)
    n_chunks = seq // RET_CHUNK
    blk = pl.BlockSpec((seq, RET_DIM), lambda b, h: (b, h))
    return pl.pallas_call(
        _ret_kernel,
        grid=(batch, RET_HEADS),
        in_specs=[blk, pl.BlockSpec((1, n_chunks, RET_DIM, RET_CHUNK), lambda b, h: (h, b, 0, 0)),
                  blk, blk, pl.BlockSpec((1, 2, LANES), lambda b, h: (h, 0, 0))],
        out_specs=blk,
        out_shape=jax.ShapeDtypeStruct((batch * seq, RET_W), BF16),
        scratch_shapes=[pltpu.VMEM((n_chunks, RET_CHUNK, RET_CHUNK), BF16),
                        pltpu.VMEM((n_chunks, 2 * RET_DIM, RET_DIM), F32),
                        pltpu.VMEM((n_chunks, RET_DIM, 2 * RET_DIM), BF16)],
        compiler_params=_cparams(("parallel", "parallel")),
        name="retention",
    )(rq, rkt, rv, rg, dec)


def _fft_tables():
    n = FFT_N1 * FFT_N2
    n2 = np.arange(FFT_N2)[:, None, None]
    k1 = np.arange(FFT_N1)[None, :, None]
    n1 = np.arange(FFT_N1 // 2)[None, None, :]
    ang = 2.0 * np.pi * ((k1 * (FFT_N2 * n1 + n2)) % n) / n
    c1, s1 = np.cos(ang), np.sin(ang)
    f1 = np.concatenate([c1, -s1], axis=1)
    f1c = np.concatenate([np.concatenate([c1, s1], axis=2),
                          np.concatenate([-s1, c1], axis=2)], axis=1)
    c3, s3 = c1.transpose(0, 2, 1), s1.transpose(0, 2, 1)
    g3c = np.concatenate([np.concatenate([c3, -s3], axis=2),
                          np.concatenate([s3, c3], axis=2)], axis=1)
    a = np.arange(FFT_N2)
    ang2 = 2.0 * np.pi * ((a[:, None] * a[None, :]) % FFT_N2) / FFT_N2
    cc, ss = np.cos(ang2), np.sin(ang2)
    g2 = np.block([[cc, ss], [-ss, cc]])
    g2i = np.block([[cc, -ss], [ss, cc]])
    return dict(f1=f1, f1c=f1c, g2=g2, g2i=g2i, g3c=g3c)


def _split_bf16(a):
    hi = jnp.asarray(a, F32).astype(BF16)
    lo = (jnp.asarray(a, F32) - hi.astype(F32)).astype(BF16)
    return hi, lo


def _dot3(a_hi, a_lo, x):
    x_hi = x.astype(BF16)
    x_lo = (x - x_hi.astype(F32)).astype(BF16)
    return _dot(a_hi, x_hi) + _dot(a_hi, x_lo) + _dot(a_lo, x_hi)


def _hymlp_kernel(z_ref, w1_ref, b1_ref, w2_ref, b2_ref, w3_ref, b3_ref, fr_ref, h_ref):
    hp = lax.Precision.HIGHEST
    fr = fr_ref[...]

    def dense(h, w_ref, b_ref):
        return jnp.sin(fr * (jnp.dot(h, w_ref[...], precision=hp, preferred_element_type=F32) + b_ref[...]))

    h_ref[...] = dense(dense(dense(z_ref[...], w1_ref, b1_ref), w2_ref, b2_ref), w3_ref, b3_ref)


def _hyfilt_kernel(h_ref, w4f_ref, w4b_ref, dl_ref, f1h_ref, f1l_ref, g2h_ref, g2l_ref, kf_ref,
                   tf_ref, tb_ref, af_ref, ab_ref):
    seq = h_ref.shape[0]
    hp = lax.Precision.HIGHEST
    blk = FFT_N2
    rows = HY_MLP_ROWS
    inv_n = 1.0 / (FFT_N1 * FFT_N2)
    dl = jnp.abs(dl_ref[...])

    def fill(i, carry):
        row0 = pl.multiple_of(i * rows, rows)
        t_idx = row0 + lax.broadcasted_iota(jnp.int32, (rows, LANES), 0)
        window = jnp.exp(-(t_idx.astype(F32) / (seq - 1)) * dl)
        h = h_ref[pl.ds(row0, rows), :]
        for w4_ref, t_ref, backward in ((w4f_ref, tf_ref, False), (w4b_ref, tb_ref, True)):
            filt = jnp.dot(h, w4_ref[...], precision=hp, preferred_element_type=F32) * window
            if backward:
                filt = jnp.where(t_idx == 0, 0.0, filt)
            for q in range(rows // blk):
                dst = pl.multiple_of((i * (rows // blk) + q) * PITCH_T, 8)
                t_ref[pl.ds(dst, blk), :] = filt[q * blk:(q + 1) * blk]
        return carry

    lax.fori_loop(0, seq // rows, fill, 0)

    def stage1(n2, carry):
        ts = pl.ds(n2, FFT_N1 // 2, stride=PITCH_T)
        r = _dot3(f1h_ref[n2], f1l_ref[n2], jnp.concatenate([tf_ref[ts, :], tb_ref[ts, :]], axis=1))
        for a_ref, ls in ((af_ref, slice(0, LANES)), (ab_ref, slice(LANES, 2 * LANES))):
            a_ref[pl.ds(n2, FFT_N1, stride=PITCH_A), :] = r[:FFT_N1, ls]
            a_ref[pl.ds(FFT_N2 + n2, FFT_N1, stride=PITCH_A), :] = r[FFT_N1:, ls]
        return carry

    lax.fori_loop(0, FFT_N2, stage1, 0, unroll=FFT_UNROLL)

    def stage2(j, carry):
        k1s = [j * 2, j * 2 + 1]
        blocks = [pl.ds(pl.multiple_of(k1 * PITCH_A, 8), 2 * FFT_N2) for k1 in k1s]
        w = jnp.concatenate([ref[rs, :] for rs in blocks for ref in (af_ref, ab_ref)], axis=1)
        y = _dot3(g2h_ref[...], g2l_ref[...], w) * inv_n
        for g, k1 in enumerate(k1s):
            yf = y[:, (2 * g) * LANES:(2 * g + 1) * LANES]
            yb = y[:, (2 * g + 1) * LANES:(2 * g + 2) * LANES]
            kf_ref[k1] = jnp.concatenate([yf[:FFT_N2] + yb[:FFT_N2], yf[FFT_N2:] - yb[FFT_N2:]], axis=0)
        return carry

    lax.fori_loop(0, FFT_N1 // 2, stage2, 0, unroll=2)


def _hyena_filter_spectrum(w1, b1, w2, b2, w3, b3, w4, freq, seq, tables):
    f1, g2 = tables["f1"], tables["g2"]
    t = jnp.arange(seq, dtype=F32) / (seq - 1)
    bands = (HY_EMB_DIM - 1) // 2
    w = 2.0 * math.pi * jnp.arange(seq, dtype=F32) / seq
    f = jnp.linspace(1e-4, bands - 1, bands, dtype=F32)
    fw = f[None, :] * w[:, None]
    z = jnp.concatenate([t[:, None], jnp.cos(fw), -jnp.sin(fw)], axis=-1)
    z = jnp.pad(z, ((0, 0), (0, LANES - HY_EMB_DIM)))
    w1p = jnp.pad(w1, ((0, LANES - HY_EMB_DIM), (0, 0)))
    max_decay = math.log(HY_TARGET) / HY_FAST_DECAY
    min_decay = math.log(HY_TARGET) / HY_SLOW_DECAY
    deltas = jnp.linspace(min_decay, max_decay, HY_WIDTH, dtype=F32)[None, :]
    f1h, f1l = _split_bf16(f1)
    g2h, g2l = _split_bf16(g2)
    n_ct = HY_WIDTH // LANES
    hid = HY_FILTER_HIDDEN
    rows_t = (FFT_N1 // 2) * PITCH_T
    rows_a = FFT_N1 * PITCH_A
    h3 = pl.pallas_call(
        _hymlp_kernel,
        grid=(seq // HY_MLP_ROWS,),
        in_specs=[pl.BlockSpec((HY_MLP_ROWS, LANES), lambda i: (i, 0)), _const_spec((LANES, hid)),
                  _const_spec((1, hid)), _const_spec((hid, hid)), _const_spec((1, hid)),
                  _const_spec((hid, hid)), _const_spec((1, hid)), _const_spec((1, hid))],
        out_specs=pl.BlockSpec((HY_MLP_ROWS, hid), lambda i: (i, 0)),
        out_shape=jax.ShapeDtypeStruct((seq, hid), F32),
        compiler_params=_cparams(("parallel",)),
        name="hyena_mlp",
    )(z, w1p, b1[None, :], w2, b2[None, :], w3, b3[None, :], freq[None, :])
    return pl.pallas_call(
        _hyfilt_kernel,
        grid=(n_ct,),
        in_specs=[_const_spec((seq, hid)),
                  pl.BlockSpec((hid, LANES), lambda c: (0, c)),
                  pl.BlockSpec((hid, LANES), lambda c: (0, n_ct + c)),
                  pl.BlockSpec((1, LANES), lambda c: (0, c)),
                  _const_spec(f1h.shape), _const_spec(f1l.shape), _const_spec(g2h.shape),
                  _const_spec(g2l.shape)],
        out_specs=pl.BlockSpec((FFT_N1, 2 * FFT_N2, LANES), lambda c: (0, 0, c)),
        out_shape=jax.ShapeDtypeStruct((FFT_N1, 2 * FFT_N2, HY_WIDTH), F32),
        scratch_shapes=[pltpu.VMEM((rows_t, LANES), F32), pltpu.VMEM((rows_t, LANES), F32),
                        pltpu.VMEM((rows_a, LANES), F32), pltpu.VMEM((rows_a, LANES), F32)],
        compiler_params=_cparams(("arbitrary",)),
        name="hyena_filter",
    )(h3, w4, w4, deltas, f1h, f1l, g2h, g2l)


def _hyconv_kernel(u_ref, z0_ref, bias_ref, kf_ref, f1_ref, g2_ref, g2i_ref, g3_ref, o_ref,
                   ua_ref, ub_ref, a_ref, b_ref):
    blk = FFT_N2
    half = FFT_N1 // 2
    scratch = (ua_ref, ub_ref)

    def fill(n1, carry):
        src = pl.ds(pl.multiple_of(n1 * blk, blk), blk)
        dst = pl.ds(pl.multiple_of(n1 * PITCH_T, 8), blk)
        for s, t_ref in enumerate(scratch):
            t_ref[dst, :] = u_ref[s, src, :].astype(F32)
        return carry

    lax.fori_loop(0, half, fill, 0, unroll=FFT_UNROLL)

    def stage1(n2, carry):
        xs = jnp.concatenate([ua_ref[pl.ds(n2, half, stride=PITCH_T), :],
                              ub_ref[pl.ds(n2, half, stride=PITCH_T), :]], axis=0).astype(BF16)
        r = _dot(f1_ref[n2], xs)
        a_ref[pl.ds(n2, FFT_N1, stride=PITCH_A), :] = r[:FFT_N1]
        a_ref[pl.ds(FFT_N2 + n2, FFT_N1, stride=PITCH_A), :] = r[FFT_N1:]
        return carry

    lax.fori_loop(0, FFT_N2, stage1, 0, unroll=FFT_UNROLL)

    def group_blocks(j):
        k1s = [j * FFT_GROUP + g for g in range(FFT_GROUP)]
        return k1s, [pl.ds(pl.multiple_of(k1 * PITCH_A, 8), 2 * FFT_N2) for k1 in k1s]

    def stage2_fwd(j, carry):
        k1s, blocks = group_blocks(j)
        w = jnp.concatenate([a_ref[rs, :].astype(BF16) for rs in blocks], axis=1)
        y = _dot(g2_ref[...], w)
        kf = jnp.concatenate([kf_ref[k1] for k1 in k1s], axis=1)
        yr, yi = y[:FFT_N2], y[FFT_N2:]
        kr, ki = kf[:FFT_N2], kf[FFT_N2:]
        z = jnp.concatenate([yr * kr - yi * ki, yr * ki + yi * kr], axis=0)
        for g, rs in enumerate(blocks):
            a_ref[rs, :] = z[:, g * LANES:(g + 1) * LANES]
        return carry

    lax.fori_loop(0, FFT_N1 // FFT_GROUP, stage2_fwd, 0, unroll=2)

    def stage2_inv(j, carry):
        k1s, blocks = group_blocks(j)
        z = jnp.concatenate([a_ref[rs, :].astype(BF16) for rs in blocks], axis=1)
        bv = _dot(g2i_ref[...], z)
        for g, k1 in enumerate(k1s):
            ls = slice(g * LANES, (g + 1) * LANES)
            b_ref[pl.ds(k1, FFT_N2, stride=PITCH_B), :] = bv[:FFT_N2, ls]
            b_ref[pl.ds(FFT_N1 + k1, FFT_N2, stride=PITCH_B), :] = bv[FFT_N2:, ls]
        return carry

    lax.fori_loop(0, FFT_N1 // FFT_GROUP, stage2_inv, 0, unroll=2)
    bias = bias_ref[...]

    def stage3(n2, carry):
        rs = pl.ds(pl.multiple_of(n2 * PITCH_B, 8), 2 * FFT_N1)
        y = _dot(g3_ref[n2], b_ref[rs, :].astype(BF16))
        ts = pl.ds(n2, half, stride=PITCH_T)
        ua_ref[ts, :] = y[:half] + ua_ref[ts, :] * bias
        ub_ref[ts, :] = y[half:] + ub_ref[ts, :] * bias
        return carry

    lax.fori_loop(0, FFT_N2, stage3, 0, unroll=FFT_UNROLL)

    def finish(n1, carry):
        src = pl.ds(pl.multiple_of(n1 * PITCH_T, 8), blk)
        dst = pl.ds(pl.multiple_of(n1 * blk, blk), blk)
        for s, t_ref in enumerate(scratch):
            o_ref[s, dst, :] = (t_ref[src, :] * z0_ref[s, dst, :].astype(F32)).astype(BF16)
        return carry

    lax.fori_loop(0, half, finish, 0, unroll=FFT_UNROLL)


def _hyena_conv(u, z0, hy_bias, kf, batch, seq, tables):
    n_ct = HY_WIDTH // LANES
    rows_t = (FFT_N1 // 2) * PITCH_T
    pair = pl.BlockSpec((2, seq, LANES), lambda c, b: (b, 0, c))
    const = lambda name: jnp.asarray(tables[name], F32).astype(BF16)
    f1c, g2, g2i, g3c = const("f1c"), const("g2"), const("g2i"), const("g3c")
    y = pl.pallas_call(
        _hyconv_kernel,
        grid=(n_ct, batch // 2),
        in_specs=[pair, pair, pl.BlockSpec((1, LANES), lambda c, b: (0, c)),
                  pl.BlockSpec((FFT_N1, 2 * FFT_N2, LANES), lambda c, b: (0, 0, c),
                               pipeline_mode=pl.Buffered(1)),
                  _const_spec(f1c.shape), _const_spec(g2.shape), _const_spec(g2i.shape),
                  _const_spec(g3c.shape)],
        out_specs=pair,
        out_shape=jax.ShapeDtypeStruct((batch, seq, HY_WIDTH), BF16),
        scratch_shapes=[pltpu.VMEM((rows_t, LANES), F32), pltpu.VMEM((rows_t, LANES), F32),
                        pltpu.VMEM((FFT_N1 * PITCH_A, LANES), F32),
                        pltpu.VMEM((FFT_N2 * PITCH_B, LANES), F32)],
        compiler_params=_cparams(("arbitrary", "arbitrary")),
        name="hyena_conv",
    )(u.reshape(batch, seq, HY_WIDTH), z0.reshape(batch, seq, HY_WIDTH), hy_bias[None, :], kf,
      f1c, g2, g2i, g3c)
    return y.reshape(batch * seq, HY_WIDTH)


def kernel(x, mix_pre_norm, mix_post_norm, ffn_pre_norm, ffn_post_norm, ffn_w_gate, ffn_w_up, ffn_conv, ffn_w_down, a_w_in, a_q_norm, a_w_q_up, a_kv_norm, a_w_kv_up, a_rpb, a_w_out, c_w_in, c_decay_fwd, c_decay_bwd, c_short_conv, c_filt_w1, c_filt_b1, c_filt_w2, c_filt_b2, c_filt_w3, c_filt_b3, c_filt_w4, c_filt_freq, c_hy_bias, c_w_out):
    batch, seq, d = x.shape
    assert d == D_MODEL and seq * 2 == FFT_N1 * FFT_N2 and seq % ROW_TILE == 0 and batch % 2 == 0
    x2 = x.reshape(batch * seq, d)

    q, k, v2, nq, nk, nv = _front0(x2, mix_pre_norm[0], a_w_in[0], a_q_norm[0], a_w_q_up[0],
                                   a_kv_norm[0], a_w_kv_up[0], seq)
    a = _mla_attention(q, k, v2, batch, seq)
    b = _na_attention(nq, nk, nv, a_rpb[0], batch, seq)
    x2 = _mix_ffn(a, b, a_w_out[0], x2, mix_post_norm[0], ffn_pre_norm[0], ffn_w_gate[0], ffn_w_up[0],
                  ffn_conv[0], ffn_w_down[0], ffn_post_norm[0], seq)

    rq, rkt, rv, rg, u, z0 = _front1(x2, mix_pre_norm[1], c_w_in[0], c_short_conv[0], seq)
    c = _retention(rq, rkt, rv, rg, c_decay_fwd[0], c_decay_bwd[0], batch, seq)
    tables = _fft_tables()
    kf = _hyena_filter_spectrum(c_filt_w1[0], c_filt_b1[0], c_filt_w2[0], c_filt_b2[0], c_filt_w3[0],
                                c_filt_b3[0], c_filt_w4[0], c_filt_freq[0], seq, tables)
    dd = _hyena_conv(u, z0, c_hy_bias[0], kf, batch, seq, tables)
    x2 = _mix_ffn(c, dd, c_w_out[0], x2, mix_post_norm[1], ffn_pre_norm[1], ffn_w_gate[1], ffn_w_up[1],
                  ffn_conv[1], ffn_w_down[1], ffn_post_norm[1], seq)
    return x2.reshape(batch, seq, d)
```

```python
import functools
import math

import numpy as np
import jax
import jax.numpy as jnp
from jax import lax
from jax.experimental import pallas as pl
from jax.experimental.pallas import tpu as pltpu

F32 = jnp.float32
BF16 = jnp.bfloat16

D_MODEL = 1024
GRID_W = 64
MLA_HEADS = 8
MLA_Q_LORA = 256
MLA_KV_LORA = 128
MLA_NOPE = 64
MLA_ROPE = 32
MLA_V = 64
ROPE_THETA = 10000.0
NA_HEADS = 8
NA_HEAD_DIM = 64
NA_WIN_H = 8
NA_WIN_W = 16
NA_WIDTH = NA_HEADS * NA_HEAD_DIM
RET_HEADS = 4
RET_DIM = 128
RET_CHUNK = 128
RET_W = RET_HEADS * RET_DIM
HY_WIDTH = 512
HY_EMB_DIM = 33
HY_FILTER_HIDDEN = 64
HY_FAST_DECAY = 0.3
HY_SLOW_DECAY = 1.5
HY_TARGET = 1e-2
D_FF = 2816
EPS = 1e-6
LOG2E = math.log2(math.e)

LANES = 128
BF16_SUBLANES = 16
VMEM_LIMIT = 56 * 1024 * 1024

ROW_TILE = 512
MLA_Q_TILE = 1024
MLA_SUB_TILE = 256
MLA_K_CHUNK = 512
NA_ROWS = 4
NA_KEY_ROWS = 12
NA_K_CHUNK = 256
FFN_CHUNK = 256
RET_UNROLL = 4
FFT_UNROLL = 4
FFT_GROUP = 4
HY_MLP_ROWS = 256
HALO = BF16_SUBLANES

FFT_N1 = 128
FFT_N2 = 64
PITCH_T = 72
PITCH_A = 136
PITCH_B = 264


def _cparams(sem, vmem_limit=VMEM_LIMIT):
    return pltpu.CompilerParams(dimension_semantics=sem, vmem_limit_bytes=vmem_limit)


def _rms(x, g):
    return x * lax.rsqrt(jnp.mean(x * x, axis=-1, keepdims=True) + EPS) * g


def _dot(a, b):
    return jnp.dot(a, b, preferred_element_type=F32)


def _dot_nt(a, b):
    return lax.dot_general(a, b, (((1,), (1,)), ((), ())), preferred_element_type=F32)


def _const_spec(shape):
    nd = len(shape)
    return pl.BlockSpec(shape, lambda *_: (0,) * nd, pipeline_mode=pl.Buffered(1))


def _halo_specs(tm, t_rows, width):
    per = tm // HALO
    last = t_rows // HALO - 1
    return [pl.BlockSpec((tm, width), lambda i: (i, 0)),
            pl.BlockSpec((HALO, width), lambda i: (jnp.maximum(i * per - 1, 0), 0)),
            pl.BlockSpec((HALO, width), lambda i: (jnp.minimum((i + 1) * per, last), 0))]


def _rope_tables(length, dim):
    inv = ROPE_THETA ** (-jnp.arange(0, dim, 2, dtype=F32) / dim)
    ang = jnp.arange(length, dtype=F32)[:, None] * inv[None, :]
    return jnp.cos(ang), jnp.sin(ang)


def _front0_kernel(x_ref, g_ref, wc_ref, wn_ref, qn_ref, kvn_ref, wq_ref, wqr_ref, wkv_ref,
                   wv_ref, vone_ref, tab_ref, q_ref, k_ref, v_ref, nq_ref, nk_ref, nv_ref):
    hn = _rms(x_ref[...], g_ref[...]).astype(BF16)
    c = _dot(hn, wc_ref[...])
    n = _dot(hn, wn_ref[...])
    nq_ref[...] = (n[:, :NA_WIDTH] * (NA_HEAD_DIM ** -0.5 * LOG2E)).astype(BF16)
    nk_ref[...] = n[:, NA_WIDTH:2 * NA_WIDTH].astype(BF16)
    nv_ref[...] = n[:, 2 * NA_WIDTH:].astype(BF16)
    cqn = _rms(c[:, :MLA_Q_LORA], qn_ref[...]).astype(BF16)
    ckvn = _rms(c[:, MLA_Q_LORA:MLA_Q_LORA + MLA_KV_LORA], kvn_ref[...]).astype(BF16)
    kr = c[:, 384:512]
    krr = c[:, 512:640]
    cosq = tab_ref[:, 0:128]
    sinq = tab_ref[:, 128:256]
    cosk = tab_ref[:, 256:384]
    sink = tab_ref[:, 384:512]
    k_rope = kr * cosk + krr * sink
    q = _dot(cqn, wq_ref[...])
    qr = _dot(cqn, wqr_ref[...])
    kn = _dot(ckvn, wkv_ref[...])
    for h in range(MLA_HEADS):
        sl = slice(h * LANES, (h + 1) * LANES)
        q_ref[:, sl] = (q[:, sl] * cosq + qr[:, sl] * sinq).astype(BF16)
        k_ref[:, sl] = (kn[:, sl] + k_rope).astype(BF16)
    v_ref[...] = (_dot(ckvn, wv_ref[...]) + vone_ref[...]).astype(BF16)


def _front0(x2, g, a_w_in, a_q_norm, a_w_q_up, a_kv_norm, a_w_kv_up, seq):
    t_rows = x2.shape[0]
    tm = ROW_TILE
    n_seq = seq // tm
    w = a_w_in
    wkr = w[:, 384:416]
    zeros = lambda n: jnp.zeros((D_MODEL, n), F32)
    wkr_full = jnp.concatenate([zeros(64), wkr, zeros(32)], axis=1)
    wkr_rot = jnp.concatenate([zeros(64), -wkr[:, 16:], wkr[:, :16], zeros(32)], axis=1)
    wc = jnp.concatenate([w[:, :384], wkr_full, wkr_rot], axis=1).astype(BF16)
    wn = w[:, 416:].astype(BF16)
    wq3 = a_w_q_up.reshape(MLA_Q_LORA, MLA_HEADS, MLA_NOPE + MLA_ROPE)
    nope, rope = wq3[..., :MLA_NOPE], wq3[..., MLA_NOPE:]
    pad32 = jnp.zeros((MLA_Q_LORA, MLA_HEADS, 32), F32)
    pad64 = jnp.zeros((MLA_Q_LORA, MLA_HEADS, 64), F32)
    wq = jnp.concatenate([nope, rope, pad32], axis=-1).reshape(MLA_Q_LORA, -1).astype(BF16)
    wqr = jnp.concatenate([pad64, -rope[..., 16:], rope[..., :16], pad32],
                          axis=-1).reshape(MLA_Q_LORA, -1).astype(BF16)
    wkv3 = a_w_kv_up.reshape(MLA_KV_LORA, MLA_HEADS, MLA_NOPE + MLA_V)
    knope, vup = wkv3[..., :MLA_NOPE], wkv3[..., MLA_NOPE:]
    kpad = jnp.zeros((MLA_KV_LORA, MLA_HEADS, 64), F32)
    wkv = jnp.concatenate([knope, kpad], axis=-1).reshape(MLA_KV_LORA, -1).astype(BF16)
    vup4 = vup.reshape(MLA_KV_LORA, MLA_HEADS // 2, 2, MLA_V)
    vpad = jnp.zeros((MLA_KV_LORA, MLA_HEADS // 2, 64), F32)
    wv = jnp.concatenate([vup4[:, :, 0], vpad, vpad, vup4[:, :, 1]], axis=-1)
    wv = wv.reshape(MLA_KV_LORA, -1).astype(BF16)
    pair_one = np.zeros((256,), np.float32)
    pair_one[64] = 1.0
    pair_one[128] = 1.0
    vone = jnp.asarray(np.tile(pair_one, MLA_HEADS // 2)[None, :])
    cos, sin = _rope_tables(seq, MLA_ROPE)
    sc = (MLA_NOPE + MLA_ROPE) ** -0.5 * LOG2E
    z16 = jnp.zeros((seq, 32), F32)
    z64 = jnp.zeros((seq, 64), F32)
    tab = jnp.concatenate([
        jnp.full((seq, 64), sc, F32), cos * sc, cos * sc, z16,
        z64, sin * sc, sin * sc, z16,
        z64, cos, cos, z16,
        z64, sin, sin, z16], axis=1)
    out_shapes = [jax.ShapeDtypeStruct((t_rows, 1024), BF16)] * 3 + \
                 [jax.ShapeDtypeStruct((t_rows, NA_WIDTH), BF16)] * 3
    row = lambda n: pl.BlockSpec((tm, n), lambda i: (i, 0))
    return pl.pallas_call(
        _front0_kernel,
        grid=(t_rows // tm,),
        in_specs=[row(D_MODEL), _const_spec((1, D_MODEL)), _const_spec(wc.shape), _const_spec(wn.shape),
                  _const_spec((1, MLA_Q_LORA)), _const_spec((1, MLA_KV_LORA)), _const_spec(wq.shape),
                  _const_spec(wqr.shape), _const_spec(wkv.shape), _const_spec(wv.shape),
                  _const_spec((1, 1024)),
                  pl.BlockSpec((tm, 512), lambda i: (i % n_seq, 0))],
        out_specs=[row(1024), row(1024), row(1024), row(NA_WIDTH), row(NA_WIDTH), row(NA_WIDTH)],
        out_shape=out_shapes,
        compiler_params=_cparams(("parallel",)),
        name="front0",
    )(x2, g[None, :], wc, wn, a_q_norm[None, :], a_kv_norm[None, :], wq, wqr, wkv, wv, vone, tab)


def _pair_normalise(acc_even, acc_odd):
    lane = lax.broadcasted_iota(jnp.int32, acc_even.shape, 1)
    return jnp.where(lane < 64, acc_even / acc_even[:, 64:65], acc_odd / acc_odd[:, 0:1])


def _staggered(units, bufs, scores, weighted_values):
    accs = {}
    pending = None
    for i, unit in enumerate(units):
        s_ref = bufs[i % len(bufs)]
        mrow = scores(unit, s_ref)
        if pending is not None:
            accs[pending[0]] = weighted_values(*pending)
        pending = (unit, s_ref, mrow)
    accs[pending[0]] = weighted_values(*pending)
    return accs


def _mla_kernel(q_ref, k_ref, v_ref, o_ref, s0_ref, s1_ref):
    sub = MLA_SUB_TILE
    seq = k_ref.shape[0]
    n_sub = q_ref.shape[0] // sub
    chunks = [slice(c * MLA_K_CHUNK, (c + 1) * MLA_K_CHUNK) for c in range(seq // MLA_K_CHUNK)]
    head = lambda hh: slice(hh * LANES, (hh + 1) * LANES)

    def scores(unit, s_ref):
        r, hh = unit
        qh = q_ref[r * sub:(r + 1) * sub, head(hh)]
        m = jnp.full((sub, LANES), -jnp.inf, F32)
        for cs in chunks:
            s = _dot_nt(qh, k_ref[cs, head(hh)])
            s_ref[:, cs] = s
            for j in range(MLA_K_CHUNK // LANES):
                m = jnp.maximum(m, s[:, j * LANES:(j + 1) * LANES])
        return jnp.max(m, axis=-1, keepdims=True)

    def weighted_values(unit, s_ref, mrow):
        _, hh = unit
        acc = jnp.zeros((sub, LANES), F32)
        for cs in chunks:
            p = jnp.exp2(s_ref[:, cs] - mrow).astype(BF16)
            acc = acc + _dot(p, v_ref[cs, head(hh)])
        return acc

    units = [(r, hh) for r in range(n_sub) for hh in range(2)]
    accs = _staggered(units, (s0_ref, s1_ref), scores, weighted_values)
    for r in range(n_sub):
        o_ref[r * sub:(r + 1) * sub, :] = _pair_normalise(accs[(r, 0)], accs[(r, 1)]).astype(BF16)


def _mla_attention(q, k, v2, batch, seq):
    tq = MLA_Q_TILE
    nq = seq // tq
    return pl.pallas_call(
        _mla_kernel,
        grid=(batch, MLA_HEADS // 2, nq),
        in_specs=[pl.BlockSpec((tq, 256), lambda b, p, i: (b * nq + i, p)),
                  pl.BlockSpec((seq, 256), lambda b, p, i: (b, p)),
                  pl.BlockSpec((seq, 256), lambda b, p, i: (b, p))],
        out_specs=pl.BlockSpec((tq, LANES), lambda b, p, i: (b * nq + i, p)),
        out_shape=jax.ShapeDtypeStruct((batch * seq, MLA_HEADS * MLA_V), BF16),
        scratch_shapes=[pltpu.VMEM((MLA_SUB_TILE, seq), F32), pltpu.VMEM((MLA_SUB_TILE, seq), F32)],
        compiler_params=_cparams(("parallel", "parallel", "arbitrary")),
        name="mla_attention",
    )(q, k, v2)


def _na_bias_table(rpb, rows):
    kh, kw = NA_WIN_H, NA_WIN_W
    neg = -1e30
    c = np.arange(GRID_W)[:, None]
    kc = np.arange(GRID_W)[None, :]
    cst = np.clip(c - kw // 2, 0, GRID_W - kw)
    col_valid = (kc >= cst) & (kc < cst + kw)
    col_off = kc - c + (kw - 1)
    onehot = ((col_off[None] == np.arange(2 * kw - 1)[:, None, None]) & col_valid[None]).astype(np.float32)
    blocks = jnp.einsum('hdj,jck->hdck', rpb.astype(F32), jnp.asarray(onehot),
                        precision=lax.Precision.HIGHEST)
    blocks = jnp.where(col_valid, blocks, neg)
    masked = jnp.full((NA_HEADS, GRID_W, GRID_W), neg, F32)
    tables = []
    for r0, ws in ((0, 0), (2 * NA_ROWS, 2 * NA_ROWS - kh // 2), (rows - NA_ROWS, rows - NA_KEY_ROWS)):
        q_rows = []
        for ri in range(NA_ROWS):
            r = r0 + ri
            rs = min(max(r - kh // 2, 0), rows - kh)
            k_blocks = []
            for i in range(NA_KEY_ROWS):
                kr = ws + i
                k_blocks.append(blocks[:, kr - r + (kh - 1)] if rs <= kr < rs + kh else masked)
            q_rows.append(jnp.concatenate(k_blocks, axis=-1))
        tables.append(jnp.concatenate(q_rows, axis=1))
    return jnp.stack(tables)


def _na_kernel(q_ref, k_ref, v_ref, b_ref, o_ref, s0_ref, s1_ref):
    g = pl.program_id(1)
    rows = k_ref.shape[0] // GRID_W
    ws = jnp.clip(g * NA_ROWS - NA_WIN_H // 2, 0, rows - NA_KEY_ROWS)
    start = pl.multiple_of(ws * GRID_W, GRID_W)
    nkeys = NA_KEY_ROWS * GRID_W
    nqry = NA_ROWS * GRID_W
    lane = lax.broadcasted_iota(jnp.int32, (1, LANES), 1)
    pair = lambda p: slice(p * LANES, (p + 1) * LANES)
    chunks = [slice(c * NA_K_CHUNK, (c + 1) * NA_K_CHUNK) for c in range(nkeys // NA_K_CHUNK)]

    def scores(unit, s_ref):
        p, hh = unit
        q2 = q_ref[:, pair(p)]
        own = (lane < 64) if hh == 0 else (lane >= 64)
        qm = jnp.where(own, q2, jnp.zeros_like(q2))
        m = jnp.full((nqry, LANES), -jnp.inf, F32)
        for cs in chunks:
            k2 = k_ref[pl.ds(pl.multiple_of(start + cs.start, GRID_W), NA_K_CHUNK), pair(p)]
            s = _dot_nt(qm, k2) + b_ref[0, 2 * p + hh, :, cs]
            s_ref[:, cs] = s
            for j in range(NA_K_CHUNK // LANES):
                m = jnp.maximum(m, s[:, j * LANES:(j + 1) * LANES])
        return jnp.max(m, axis=-1, keepdims=True)

    def weighted_values(unit, s_ref, mrow):
        p, hh = unit
        own = (lane < 64) if hh == 0 else (lane >= 64)
        ones_lane = 64 if hh == 0 else 0
        acc = jnp.zeros((nqry, LANES), F32)
        for cs in chunks:
            v2 = v_ref[pl.ds(pl.multiple_of(start + cs.start, GRID_W), NA_K_CHUNK), pair(p)]
            vm = jnp.where(own, v2, jnp.where(lane == ones_lane, 1.0, 0.0).astype(BF16))
            acc = acc + _dot(jnp.exp2(s_ref[:, cs] - mrow).astype(BF16), vm)
        return acc

    units = [(p, hh) for p in range(NA_HEADS // 2) for hh in range(2)]
    accs = _staggered(units, (s0_ref, s1_ref), scores, weighted_values)
    for p in range(NA_HEADS // 2):
        o_ref[:, pair(p)] = _pair_normalise(accs[(p, 0)], accs[(p, 1)]).astype(BF16)


def _na_attention(nq, nk, nv, rpb, batch, seq):
    rows = seq // GRID_W
    groups = rows // NA_ROWS
    nqry = NA_ROWS * GRID_W
    nkeys = NA_KEY_ROWS * GRID_W
    bias = _na_bias_table(rpb * LOG2E, rows)
    case = lambda g: (g > 0).astype(jnp.int32) + (g == groups - 1).astype(jnp.int32)
    return pl.pallas_call(
        _na_kernel,
        grid=(batch, groups),
        in_specs=[pl.BlockSpec((nqry, NA_WIDTH), lambda b, g: (b * groups + g, 0)),
                  pl.BlockSpec((seq, NA_WIDTH), lambda b, g: (b, 0)),
                  pl.BlockSpec((seq, NA_WIDTH), lambda b, g: (b, 0)),
                  pl.BlockSpec((1, NA_HEADS, nqry, nkeys), lambda b, g: (case(g), 0, 0, 0))],
        out_specs=pl.BlockSpec((nqry, NA_WIDTH), lambda b, g: (b * groups + g, 0)),
        out_shape=jax.ShapeDtypeStruct((batch * seq, NA_WIDTH), BF16),
        scratch_shapes=[pltpu.VMEM((nqry, nkeys), F32), pltpu.VMEM((nqry, nkeys), F32)],
        compiler_params=_cparams(("parallel", "arbitrary")),
        name="na_attention",
    )(nq, nk, nv, bias)


def _gelu_tanh(x):
    return 0.5 * x * (1.0 + jnp.tanh(math.sqrt(2.0 / math.pi) * (x + 0.044715 * (x * x * x))))


def _mix_ffn_kernel(a_ref, ap_ref, an_ref, b_ref, bp_ref, bn_ref, x_ref, xp_ref, xn_ref, wa_ref, wb_ref,
                    gmix_ref, gpre_ref, wg_ref, wu_ref, cw_ref, wd_ref, gpost_ref, o_ref, act_ref, *, n_seq):
    i = pl.program_id(0)
    tm = x_ref.shape[0]
    ext = tm + 2 * HALO
    with_halo = lambda prev, mid, nxt: jnp.concatenate([prev[...], mid[...], nxt[...]], axis=0)
    mix = (_dot(with_halo(ap_ref, a_ref, an_ref), wa_ref[...])
           + _dot(with_halo(bp_ref, b_ref, bn_ref), wb_ref[...]))
    x1 = with_halo(xp_ref, x_ref, xn_ref) + _rms(mix, gmix_ref[...])
    hn = _rms(x1, gpre_ref[...])
    r = lax.broadcasted_iota(jnp.int32, (ext, 1), 0)
    outside = ((r < HALO) & (i % n_seq == 0)) | ((r >= HALO + tm) & (i % n_seq == n_seq - 1))
    h_ext = jnp.where(outside, 0.0, hn).astype(BF16)
    hb = h_ext[HALO:HALO + tm]
    for j in range(D_FF // FFN_CHUNK):
        cs = slice(j * FFN_CHUNK, (j + 1) * FFN_CHUNK)
        gate = _dot(h_ext, wg_ref[:, cs])
        up = _dot(hb, wu_ref[:, cs])
        g_prev = pltpu.roll(gate, 1, 0)[HALO:HALO + tm]
        g_next = pltpu.roll(gate, ext - 1, 0)[HALO:HALO + tm]
        conv = g_prev * cw_ref[0:1, cs] + gate[HALO:HALO + tm] * cw_ref[1:2, cs] + g_next * cw_ref[2:3, cs]
        act_ref[:, cs] = (_gelu_tanh(conv) * up).astype(BF16)
    f = _dot(act_ref[...], wd_ref[...])
    o_ref[...] = x1[HALO:HALO + tm] + _rms(f, gpost_ref[...])


def _mix_ffn(a, b, w_out, x2, gmix, gpre, w_gate, w_up, conv_w, w_down, gpost, seq):
    t_rows = x2.shape[0]
    tm = ROW_TILE
    n_seq = seq // tm
    half = w_out.shape[0] // 2
    tile = lambda n: _halo_specs(tm, t_rows, n)
    return pl.pallas_call(
        functools.partial(_mix_ffn_kernel, n_seq=n_seq),
        grid=(t_rows // tm,),
        in_specs=tile(half) + tile(half) + tile(D_MODEL) + [
            _const_spec((half, D_MODEL)), _const_spec((half, D_MODEL)), _const_spec((1, D_MODEL)),
            _const_spec((1, D_MODEL)), _const_spec((D_MODEL, D_FF)), _const_spec((D_MODEL, D_FF)),
            _const_spec((3, D_FF)), _const_spec((D_FF, D_MODEL)), _const_spec((1, D_MODEL))],
        out_specs=pl.BlockSpec((tm, D_MODEL), lambda i: (i, 0)),
        out_shape=jax.ShapeDtypeStruct((t_rows, D_MODEL), F32),
        scratch_shapes=[pltpu.VMEM((tm, D_FF), BF16)],
        compiler_params=_cparams(("parallel",)),
        name="mix_ffn",
    )(a, a, a, b, b, b, x2, x2, x2, w_out[:half].astype(BF16), w_out[half:].astype(BF16), gmix[None, :],
      gpre[None, :], w_gate.astype(BF16), w_up.astype(BF16), conv_w, w_down.astype(BF16), gpost[None, :])


def _front1_kernel(x_ref, xp_ref, xn_ref, g_ref, w_ref, sc_ref, tab_ref, rq_ref, rkt_ref, rv_ref,
                   rg_ref, u_ref, z0_ref, *, n_seq):
    i = pl.program_id(0)
    tm = x_ref.shape[0]
    ext = tm + 2 * HALO
    g = g_ref[...]
    hn = _rms(x_ref[...], g).astype(BF16)
    r = _dot(hn, w_ref[:, :4 * RET_W])
    cosr = tab_ref[:, 0:128]
    sinr = tab_ref[:, 128:256]
    kscale = RET_DIM ** -0.5
    c = RET_CHUNK
    for h in range(RET_HEADS):
        qs = slice(h * RET_DIM, (h + 1) * RET_DIM)
        ks = slice(RET_W + h * RET_DIM, RET_W + (h + 1) * RET_DIM)
        qh = r[:, qs]
        kh = r[:, ks]
        rq_ref[:, qs] = (qh * cosr + pltpu.roll(qh, RET_DIM // 2, 1) * sinr).astype(BF16)
        krot = (kh * cosr + pltpu.roll(kh, RET_DIM // 2, 1) * sinr) * kscale
        for j in range(tm // c):
            rkt_ref[h, j] = krot[j * c:(j + 1) * c].T.astype(BF16)
    rv_ref[...] = r[:, 2 * RET_W:3 * RET_W].astype(BF16)
    rg_ref[...] = r[:, 3 * RET_W:]
    keep_prev = jnp.where(i % n_seq == 0, 0.0, 1.0)
    keep_next = jnp.where(i % n_seq == n_seq - 1, 0.0, 1.0)
    h_ext = jnp.concatenate([(_rms(xp_ref[...], g) * keep_prev).astype(BF16), hn,
                             (_rms(xn_ref[...], g) * keep_next).astype(BF16)], axis=0)
    hy = _dot(h_ext, w_ref[:, 4 * RET_W:])
    z = (pltpu.roll(hy, 1, 0)[HALO:HALO + tm] * sc_ref[0:1, :] + hy[HALO:HALO + tm] * sc_ref[1:2, :]
         + pltpu.roll(hy, ext - 1, 0)[HALO:HALO + tm] * sc_ref[2:3, :])
    z0_ref[...] = z[:, :HY_WIDTH].astype(BF16)
    u_ref[...] = (z[:, 2 * HY_WIDTH:] * z[:, HY_WIDTH:2 * HY_WIDTH]).astype(BF16)


def _front1(x2, g, c_w_in, short_conv, seq):
    t_rows = x2.shape[0]
    tm = ROW_TILE
    n_seq = seq // tm
    cos, sin = _rope_tables(seq, RET_DIM)
    tab = jnp.concatenate([cos, cos, -sin, sin], axis=1)
    w = c_w_in.astype(BF16)
    row = lambda n: pl.BlockSpec((tm, n), lambda i: (i, 0))
    kt_shape = (RET_HEADS, t_rows // RET_CHUNK, RET_DIM, RET_CHUNK)
    kt_spec = pl.BlockSpec((RET_HEADS, tm // RET_CHUNK, RET_DIM, RET_CHUNK), lambda i: (0, i, 0, 0))
    out_shapes = [jax.ShapeDtypeStruct((t_rows, RET_W), BF16), jax.ShapeDtypeStruct(kt_shape, BF16),
                  jax.ShapeDtypeStruct((t_rows, RET_W), BF16), jax.ShapeDtypeStruct((t_rows, RET_W), F32),
                  jax.ShapeDtypeStruct((t_rows, HY_WIDTH), BF16), jax.ShapeDtypeStruct((t_rows, HY_WIDTH), BF16)]
    return pl.pallas_call(
        functools.partial(_front1_kernel, n_seq=n_seq),
        grid=(t_rows // tm,),
        in_specs=_halo_specs(tm, t_rows, D_MODEL) + [
            _const_spec((1, D_MODEL)), _const_spec(w.shape), _const_spec(short_conv.shape),
            pl.BlockSpec((tm, 256), lambda i: (i % n_seq, 0))],
        out_specs=[row(RET_W), kt_spec, row(RET_W), row(RET_W), row(HY_WIDTH), row(HY_WIDTH)],
        out_shape=out_shapes,
        compiler_params=_cparams(("parallel",)),
        name="front1",
    )(x2, x2, x2, g[None, :], w, short_conv, tab)


def _log_sigmoid(x):
    return jnp.minimum(x, 0.0) - jnp.log1p(jnp.exp(-jnp.abs(x)))


def _ret_kernel(q_ref, kt_ref, v_ref, g_ref, dec_ref, o_ref, sc_ref, kv_ref, st_ref):
    seq = q_ref.shape[0]
    c = RET_CHUNK
    n_chunks = seq // c
    lf = _log_sigmoid(dec_ref[0, 0:1, :])
    lb = _log_sigmoid(dec_ref[0, 1:2, :])
    t = lax.broadcasted_iota(jnp.int32, (c, c), 0).astype(F32)
    s_idx = lax.broadcasted_iota(jnp.int32, (c, c), 1).astype(F32)
    diff = t - s_idx
    dmat = jnp.where(diff >= 0, jnp.exp(jnp.maximum(diff, 0.0) * lf),
                     jnp.exp(jnp.maximum(-diff, 0.0) * lb))
    xi_f = jnp.exp((t + 1.0) * lf)
    xi_b = jnp.exp((c - t) * lb)
    zeta_f = jnp.exp((c - 1.0 - s_idx) * lf)
    zeta_b = jnp.exp(s_idx * lb)
    g_f = jnp.exp(c * lf)
    g_b = jnp.exp(c * lb)

    def chunk_a(n, carry):
        rs = pl.ds(pl.multiple_of(n * c, c), c)
        ktn = kt_ref[0, n]
        sc_ref[n] = (_dot(q_ref[rs, :], ktn) * dmat).astype(BF16)
        ktf = ktn.astype(F32)
        kz = jnp.concatenate([ktf * zeta_f, ktf * zeta_b], axis=0).astype(BF16)
        kv_ref[n] = _dot(kz, v_ref[rs, :])
        return carry

    lax.fori_loop(0, n_chunks, chunk_a, 0, unroll=RET_UNROLL)

    def scan_f(n, st):
        st_ref[n, :, :c] = st.astype(BF16)
        return g_f * st + kv_ref[n, :c, :]

    lax.fori_loop(0, n_chunks, scan_f, jnp.zeros((c, c), F32))

    def scan_b(m, st):
        n = n_chunks - 1 - m
        st_ref[n, :, c:] = st.astype(BF16)
        return g_b * st + kv_ref[n, c:, :]

    lax.fori_loop(0, n_chunks, scan_b, jnp.zeros((c, c), F32))

    def chunk_c(n, carry):
        rs = pl.ds(pl.multiple_of(n * c, c), c)
        cross = _dot(q_ref[rs, :], st_ref[n])
        o = _dot(sc_ref[n], v_ref[rs, :]) + cross[:, :c] * xi_f + cross[:, c:] * xi_b
        o = o * lax.rsqrt(jnp.mean(o * o, axis=-1, keepdims=True) + EPS)
        gate = g_ref[rs, :]
        o_ref[rs, :] = (o * (gate * jax.nn.sigmoid(gate))).astype(BF16)
        return carry

    lax.fori_loop(0, n_chunks, chunk_c, 0, unroll=RET_UNROLL)


def _retention(rq, rkt, rv, rg, decay_fwd, decay_bwd, batch, seq):
    dec = jnp.stack([decay_fwd, decay_bwd], axis=1)
    dec = jnp.broadcast_to(dec[:, :, None], (RET_HEADS, 2, LANES)).astype(F32)
    n_chunks = seq // RET_CHUNK
    blk = pl.BlockSpec((seq, RET_DIM), lambda b, h: (b, h))
    return pl.pallas_call(
        _ret_kernel,
        grid=(batch, RET_HEADS),
        in_specs=[blk, pl.BlockSpec((1, n_chunks, RET_DIM, RET_CHUNK), lambda b, h: (h, b, 0, 0)),
                  blk, blk, pl.BlockSpec((1, 2, LANES), lambda b, h: (h, 0, 0))],
        out_specs=blk,
        out_shape=jax.ShapeDtypeStruct((batch * seq, RET_W), BF16),
        scratch_shapes=[pltpu.VMEM((n_chunks, RET_CHUNK, RET_CHUNK), BF16),
                        pltpu.VMEM((n_chunks, 2 * RET_DIM, RET_DIM), F32),
                        pltpu.VMEM((n_chunks, RET_DIM, 2 * RET_DIM), BF16)],
        compiler_params=_cparams(("parallel", "parallel")),
        name="retention",
    )(rq, rkt, rv, rg, dec)


def _fft_tables():
    n = FFT_N1 * FFT_N2
    n2 = np.arange(FFT_N2)[:, None, None]
    k1 = np.arange(FFT_N1)[None, :, None]
    n1 = np.arange(FFT_N1 // 2)[None, None, :]
    ang = 2.0 * np.pi * ((k1 * (FFT_N2 * n1 + n2)) % n) / n
    c1, s1 = np.cos(ang), np.sin(ang)
    f1 = np.concatenate([c1, -s1], axis=1)
    f1c = np.concatenate([np.concatenate([c1, s1], axis=2),
                          np.concatenate([-s1, c1], axis=2)], axis=1)
    c3, s3 = c1.transpose(0, 2, 1), s1.transpose(0, 2, 1)
    g3c = np.concatenate([np.concatenate([c3, -s3], axis=2),
                          np.concatenate([s3, c3], axis=2)], axis=1)
    a = np.arange(FFT_N2)
    ang2 = 2.0 * np.pi * ((a[:, None] * a[None, :]) % FFT_N2) / FFT_N2
    cc, ss = np.cos(ang2), np.sin(ang2)
    g2 = np.block([[cc, ss], [-ss, cc]])
    g2i = np.block([[cc, -ss], [ss, cc]])
    return dict(f1=f1, f1c=f1c, g2=g2, g2i=g2i, g3c=g3c)


def _hymlp_kernel(z_ref, w1_ref, b1_ref, w2_ref, b2_ref, w3_ref, b3_ref, fr_ref, h_ref):
    hp = lax.Precision.HIGHEST
    fr = fr_ref[...]

    def dense(h, w_ref, b_ref):
        return jnp.sin(fr * (jnp.dot(h, w_ref[...], precision=hp, preferred_element_type=F32) + b_ref[...]))

    h_ref[...] = dense(dense(dense(z_ref[...], w1_ref, b1_ref), w2_ref, b2_ref), w3_ref, b3_ref)


def _hyfilt_kernel(h_ref, w4f_ref, w4b_ref, dl_ref, f1_ref, g2_ref, kf_ref,
                   tf_ref, tb_ref, af_ref, ab_ref):
    seq = h_ref.shape[0]
    hp = lax.Precision.HIGHEST
    blk = FFT_N2
    rows = HY_MLP_ROWS
    inv_n = 1.0 / (FFT_N1 * FFT_N2)
    dl = jnp.abs(dl_ref[...])

    def fill(i, carry):
        row0 = pl.multiple_of(i * rows, rows)
        t_idx = row0 + lax.broadcasted_iota(jnp.int32, (rows, LANES), 0)
        window = jnp.exp(-(t_idx.astype(F32) / (seq - 1)) * dl)
        h = h_ref[pl.ds(row0, rows), :]
        for w4_ref, t_ref, backward in ((w4f_ref, tf_ref, False), (w4b_ref, tb_ref, True)):
            filt = jnp.dot(h, w4_ref[...], precision=hp, preferred_element_type=F32) * window
            if backward:
                filt = jnp.where(t_idx == 0, 0.0, filt)
            for q in range(rows // blk):
                dst = pl.multiple_of((i * (rows // blk) + q) * PITCH_T, 8)
                t_ref[pl.ds(dst, blk), :] = filt[q * blk:(q + 1) * blk]
        return carry

    lax.fori_loop(0, seq // rows, fill, 0)

    def stage1(n2, carry):
        ts = pl.ds(n2, FFT_N1 // 2, stride=PITCH_T)
        r = _dot(f1_ref[n2], jnp.concatenate([tf_ref[ts, :], tb_ref[ts, :]], axis=1).astype(BF16))
        for a_ref, ls in ((af_ref, slice(0, LANES)), (ab_ref, slice(LANES, 2 * LANES))):
            a_ref[pl.ds(n2, FFT_N1, stride=PITCH_A), :] = r[:FFT_N1, ls]
            a_ref[pl.ds(FFT_N2 + n2, FFT_N1, stride=PITCH_A), :] = r[FFT_N1:, ls]
        return carry

    lax.fori_loop(0, FFT_N2, stage1, 0, unroll=FFT_UNROLL)

    def stage2(j, carry):
        k1s = [j * 2, j * 2 + 1]
        blocks = [pl.ds(pl.multiple_of(k1 * PITCH_A, 8), 2 * FFT_N2) for k1 in k1s]
        w = jnp.concatenate([ref[rs, :] for rs in blocks for ref in (af_ref, ab_ref)], axis=1)
        y = _dot(g2_ref[...], w.astype(BF16)) * inv_n
        for g, k1 in enumerate(k1s):
            yf = y[:, (2 * g) * LANES:(2 * g + 1) * LANES]
            yb = y[:, (2 * g + 1) * LANES:(2 * g + 2) * LANES]
            kf_ref[k1] = jnp.concatenate([yf[:FFT_N2] + yb[:FFT_N2], yf[FFT_N2:] - yb[FFT_N2:]], axis=0)
        return carry

    lax.fori_loop(0, FFT_N1 // 2, stage2, 0, unroll=2)


def _hyena_filter_spectrum(w1, b1, w2, b2, w3, b3, w4, freq, seq, tables):
    f1, g2 = tables["f1"], tables["g2"]
    t = jnp.arange(seq, dtype=F32) / (seq - 1)
    bands = (HY_EMB_DIM - 1) // 2
    w = 2.0 * math.pi * jnp.arange(seq, dtype=F32) / seq
    f = jnp.linspace(1e-4, bands - 1, bands, dtype=F32)
    fw = f[None, :] * w[:, None]
    z = jnp.concatenate([t[:, None], jnp.cos(fw), -jnp.sin(fw)], axis=-1)
    z = jnp.pad(z, ((0, 0), (0, LANES - HY_EMB_DIM)))
    w1p = jnp.pad(w1, ((0, LANES - HY_EMB_DIM), (0, 0)))
    max_decay = math.log(HY_TARGET) / HY_FAST_DECAY
    min_decay = math.log(HY_TARGET) / HY_SLOW_DECAY
    deltas = jnp.linspace(min_decay, max_decay, HY_WIDTH, dtype=F32)[None, :]
    f1 = jnp.asarray(f1, F32).astype(BF16)
    g2 = jnp.asarray(g2, F32).astype(BF16)
    n_ct = HY_WIDTH // LANES
    hid = HY_FILTER_HIDDEN
    rows_t = (FFT_N1 // 2) * PITCH_T
    rows_a = FFT_N1 * PITCH_A
    h3 = pl.pallas_call(
        _hymlp_kernel,
        grid=(seq // HY_MLP_ROWS,),
        in_specs=[pl.BlockSpec((HY_MLP_ROWS, LANES), lambda i: (i, 0)), _const_spec((LANES, hid)),
                  _const_spec((1, hid)), _const_spec((hid, hid)), _const_spec((1, hid)),
                  _const_spec((hid, hid)), _const_spec((1, hid)), _const_spec((1, hid))],
        out_specs=pl.BlockSpec((HY_MLP_ROWS, hid), lambda i: (i, 0)),
        out_shape=jax.ShapeDtypeStruct((seq, hid), F32),
        compiler_params=_cparams(("parallel",)),
        name="hyena_mlp",
    )(z, w1p, b1[None, :], w2, b2[None, :], w3, b3[None, :], freq[None, :])
    return pl.pallas_call(
        _hyfilt_kernel,
        grid=(n_ct,),
        in_specs=[_const_spec((seq, hid)),
                  pl.BlockSpec((hid, LANES), lambda c: (0, c)),
                  pl.BlockSpec((hid, LANES), lambda c: (0, n_ct + c)),
                  pl.BlockSpec((1, LANES), lambda c: (0, c)),
                  _const_spec(f1.shape), _const_spec(g2.shape)],
        out_specs=pl.BlockSpec((FFT_N1, 2 * FFT_N2, LANES), lambda c: (0, 0, c)),
        out_shape=jax.ShapeDtypeStruct((FFT_N1, 2 * FFT_N2, HY_WIDTH), F32),
        scratch_shapes=[pltpu.VMEM((rows_t, LANES), F32), pltpu.VMEM((rows_t, LANES), F32),
                        pltpu.VMEM((rows_a, LANES), F32), pltpu.VMEM((rows_a, LANES), F32)],
        compiler_params=_cparams(("arbitrary",)),
        name="hyena_filter",
    )(h3, w4, w4, deltas, f1, g2)


def _hyconv_kernel(u_ref, z0_ref, bias_ref, kf_ref, f1_ref, g2_ref, g2i_ref, g3_ref, o_ref,
                   ua_ref, ub_ref, a_ref, b_ref):
    blk = FFT_N2
    half = FFT_N1 // 2
    scratch = (ua_ref, ub_ref)

    def fill(n1, carry):
        src = pl.ds(pl.multiple_of(n1 * blk, blk), blk)
        dst = pl.ds(pl.multiple_of(n1 * PITCH_T, 8), blk)
        for s, t_ref in enumerate(scratch):
            t_ref[dst, :] = u_ref[s, src, :].astype(F32)
        return carry

    lax.fori_loop(0, half, fill, 0, unroll=FFT_UNROLL)

    def stage1(n2, carry):
        xs = jnp.concatenate([ua_ref[pl.ds(n2, half, stride=PITCH_T), :],
                              ub_ref[pl.ds(n2, half, stride=PITCH_T), :]], axis=0).astype(BF16)
        r = _dot(f1_ref[n2], xs)
        a_ref[pl.ds(n2, FFT_N1, stride=PITCH_A), :] = r[:FFT_N1]
        a_ref[pl.ds(FFT_N2 + n2, FFT_N1, stride=PITCH_A), :] = r[FFT_N1:]
        return carry

    lax.fori_loop(0, FFT_N2, stage1, 0, unroll=FFT_UNROLL)

    def group_blocks(j):
        k1s = [j * FFT_GROUP + g for g in range(FFT_GROUP)]
        return k1s, [pl.ds(pl.multiple_of(k1 * PITCH_A, 8), 2 * FFT_N2) for k1 in k1s]

    def stage2_fwd(j, carry):
        k1s, blocks = group_blocks(j)
        w = jnp.concatenate([a_ref[rs, :].astype(BF16) for rs in blocks], axis=1)
        y = _dot(g2_ref[...], w)
        kf = jnp.concatenate([kf_ref[k1] for k1 in k1s], axis=1)
        yr, yi = y[:FFT_N2], y[FFT_N2:]
        kr, ki = kf[:FFT_N2], kf[FFT_N2:]
        z = jnp.concatenate([yr * kr - yi * ki, yr * ki + yi * kr], axis=0)
        for g, rs in enumerate(blocks):
            a_ref[rs, :] = z[:, g * LANES:(g + 1) * LANES]
        return carry

    lax.fori_loop(0, FFT_N1 // FFT_GROUP, stage2_fwd, 0, unroll=2)

    def stage2_inv(j, carry):
        k1s, blocks = group_blocks(j)
        z = jnp.concatenate([a_ref[rs, :].astype(BF16) for rs in blocks], axis=1)
        bv = _dot(g2i_ref[...], z)
        for g, k1 in enumerate(k1s):
            ls = slice(g * LANES, (g + 1) * LANES)
            b_ref[pl.ds(k1, FFT_N2, stride=PITCH_B), :] = bv[:FFT_N2, ls]
            b_ref[pl.ds(FFT_N1 + k1, FFT_N2, stride=PITCH_B), :] = bv[FFT_N2:, ls]
        return carry

    lax.fori_loop(0, FFT_N1 // FFT_GROUP, stage2_inv, 0, unroll=2)
    bias = bias_ref[...]

    def stage3(n2, carry):
        rs = pl.ds(pl.multiple_of(n2 * PITCH_B, 8), 2 * FFT_N1)
        y = _dot(g3_ref[n2], b_ref[rs, :].astype(BF16))
        ts = pl.ds(n2, half, stride=PITCH_T)
        ua_ref[ts, :] = y[:half] + ua_ref[ts, :] * bias
        ub_ref[ts, :] = y[half:] + ub_ref[ts, :] * bias
        return carry

    lax.fori_loop(0, FFT_N2, stage3, 0, unroll=FFT_UNROLL)

    def finish(n1, carry):
        src = pl.ds(pl.multiple_of(n1 * PITCH_T, 8), blk)
        dst = pl.ds(pl.multiple_of(n1 * blk, blk), blk)
        for s, t_ref in enumerate(scratch):
            o_ref[s, dst, :] = (t_ref[src, :] * z0_ref[s, dst, :].astype(F32)).astype(BF16)
        return carry

    lax.fori_loop(0, half, finish, 0, unroll=FFT_UNROLL)


def _hyena_conv(u, z0, hy_bias, kf, batch, seq, tables):
    n_ct = HY_WIDTH // LANES
    rows_t = (FFT_N1 // 2) * PITCH_T
    pair = pl.BlockSpec((2, seq, LANES), lambda c, b: (b, 0, c))
    const = lambda name: jnp.asarray(tables[name], F32).astype(BF16)
    f1c, g2, g2i, g3c = const("f1c"), const("g2"), const("g2i"), const("g3c")
    y = pl.pallas_call(
        _hyconv_kernel,
        grid=(n_ct, batch // 2),
        in_specs=[pair, pair, pl.BlockSpec((1, LANES), lambda c, b: (0, c)),
                  pl.BlockSpec((FFT_N1, 2 * FFT_N2, LANES), lambda c, b: (0, 0, c),
                               pipeline_mode=pl.Buffered(1)),
                  _const_spec(f1c.shape), _const_spec(g2.shape), _const_spec(g2i.shape),
                  _const_spec(g3c.shape)],
        out_specs=pair,
        out_shape=jax.ShapeDtypeStruct((batch, seq, HY_WIDTH), BF16),
        scratch_shapes=[pltpu.VMEM((rows_t, LANES), F32), pltpu.VMEM((rows_t, LANES), F32),
                        pltpu.VMEM((FFT_N1 * PITCH_A, LANES), F32),
                        pltpu.VMEM((FFT_N2 * PITCH_B, LANES), F32)],
        compiler_params=_cparams(("arbitrary", "arbitrary")),
        name="hyena_conv",
    )(u.reshape(batch, seq, HY_WIDTH), z0.reshape(batch, seq, HY_WIDTH), hy_bias[None, :], kf,
      f1c, g2, g2i, g3c)
    return y.reshape(batch * seq, HY_WIDTH)


def kernel(x, mix_pre_norm, mix_post_norm, ffn_pre_norm, ffn_post_norm, ffn_w_gate, ffn_w_up, ffn_conv, ffn_w_down, a_w_in, a_q_norm, a_w_q_up, a_kv_norm, a_w_kv_up, a_rpb, a_w_out, c_w_in, c_decay_fwd, c_decay_bwd, c_short_conv, c_filt_w1, c_filt_b1, c_filt_w2, c_filt_b2, c_filt_w3, c_filt_b3, c_filt_w4, c_filt_freq, c_hy_bias, c_w_out):
    batch, seq, d = x.shape
    assert d == D_MODEL and seq * 2 == FFT_N1 * FFT_N2 and seq % ROW_TILE == 0 and batch % 2 == 0
    x2 = x.reshape(batch * seq, d)

    q, k, v2, nq, nk, nv = _front0(x2, mix_pre_norm[0], a_w_in[0], a_q_norm[0], a_w_q_up[0],
                                   a_kv_norm[0], a_w_kv_up[0], seq)
    a = _mla_attention(q, k, v2, batch, seq)
    b = _na_attention(nq, nk, nv, a_rpb[0], batch, seq)
    x2 = _mix_ffn(a, b, a_w_out[0], x2, mix_post_norm[0], ffn_pre_norm[0], ffn_w_gate[0], ffn_w_up[0],
                  ffn_conv[0], ffn_w_down[0], ffn_post_norm[0], seq)

    rq, rkt, rv, rg, u, z0 = _front1(x2, mix_pre_norm[1], c_w_in[0], c_short_conv[0], seq)
    c = _retention(rq, rkt, rv, rg, c_decay_fwd[0], c_decay_bwd[0], batch, seq)
    tables = _fft_tables()
    kf = _hyena_filter_spectrum(c_filt_w1[0], c_filt_b1[0], c_filt_w2[0], c_filt_b2[0], c_filt_w3[0],
                                c_filt_b3[0], c_filt_w4[0], c_filt_freq[0], seq, tables)
    dd = _hyena_conv(u, z0, c_hy_bias[0], kf, batch, seq, tables)
    x2 = _mix_ffn(c, dd, c_w_out[0], x2, mix_post_norm[1], ffn_pre_norm[1], ffn_w_gate[1], ffn_w_up[1],
                  ffn_conv[1], ffn_w_down[1], ffn_post_norm[1], seq)
    return x2.reshape(batch, seq, d)
```

```python
import functools
import math

import numpy as np
import jax
import jax.numpy as jnp
from jax import lax
from jax.experimental import pallas as pl
from jax.experimental.pallas import tpu as pltpu

F32 = jnp.float32
BF16 = jnp.bfloat16

D_MODEL = 1024
GRID_W = 64
MLA_HEADS = 8
MLA_Q_LORA = 256
MLA_KV_LORA = 128
MLA_NOPE = 64
MLA_ROPE = 32
MLA_V = 64
ROPE_THETA = 10000.0
NA_HEADS = 8
NA_HEAD_DIM = 64
NA_WIN_H = 8
NA_WIN_W = 16
NA_WIDTH = NA_HEADS * NA_HEAD_DIM
RET_HEADS = 4
RET_DIM = 128
RET_CHUNK = 128
RET_W = RET_HEADS * RET_DIM
HY_WIDTH = 512
HY_EMB_DIM = 33
HY_FILTER_HIDDEN = 64
HY_FAST_DECAY = 0.3
HY_SLOW_DECAY = 1.5
HY_TARGET = 1e-2
D_FF = 2816
EPS = 1e-6
LOG2E = math.log2(math.e)

LANES = 128
BF16_SUBLANES = 16
VMEM_LIMIT = 56 * 1024 * 1024

ROW_TILE = 512
MLA_Q_TILE = 1024
MLA_SUB_TILE = 256
MLA_K_CHUNK = 512
NA_ROWS = 4
NA_KEY_ROWS = 12
NA_K_CHUNK = 256
FFN_CHUNK = 256
RET_UNROLL = 8
FFT_UNROLL = 4
FFT_GROUP = 4
HY_MLP_ROWS = 256
HALO = BF16_SUBLANES

FFT_N1 = 128
FFT_N2 = 64
PITCH_T = 72
PITCH_A = 136
PITCH_B = 264


def _cparams(sem, vmem_limit=VMEM_LIMIT):
    return pltpu.CompilerParams(dimension_semantics=sem, vmem_limit_bytes=vmem_limit)


def _rms(x, g):
    return x * lax.rsqrt(jnp.mean(x * x, axis=-1, keepdims=True) + EPS) * g


def _dot(a, b):
    return jnp.dot(a, b, preferred_element_type=F32)


def _dot_nt(a, b):
    return lax.dot_general(a, b, (((1,), (1,)), ((), ())), preferred_element_type=F32)


def _const_spec(shape):
    nd = len(shape)
    return pl.BlockSpec(shape, lambda *_: (0,) * nd, pipeline_mode=pl.Buffered(1))


def _halo_specs(tm, t_rows, width):
    per = tm // HALO
    last = t_rows // HALO - 1
    return [pl.BlockSpec((tm, width), lambda i: (i, 0)),
            pl.BlockSpec((HALO, width), lambda i: (jnp.maximum(i * per - 1, 0), 0)),
            pl.BlockSpec((HALO, width), lambda i: (jnp.minimum((i + 1) * per, last), 0))]


def _rope_tables(length, dim):
    inv = ROPE_THETA ** (-jnp.arange(0, dim, 2, dtype=F32) / dim)
    ang = jnp.arange(length, dtype=F32)[:, None] * inv[None, :]
    return jnp.cos(ang), jnp.sin(ang)


def _front0_kernel(x_ref, g_ref, wc_ref, wn_ref, qn_ref, kvn_ref, wq_ref, wqr_ref, wkv_ref,
                   wv_ref, vone_ref, tab_ref, q_ref, k_ref, v_ref, nq_ref, nk_ref, nv_ref):
    hn = _rms(x_ref[...], g_ref[...]).astype(BF16)
    c = _dot(hn, wc_ref[...])
    n = _dot(hn, wn_ref[...])
    nq_ref[...] = (n[:, :NA_WIDTH] * (NA_HEAD_DIM ** -0.5 * LOG2E)).astype(BF16)
    nk_ref[...] = n[:, NA_WIDTH:2 * NA_WIDTH].astype(BF16)
    nv_ref[...] = n[:, 2 * NA_WIDTH:].astype(BF16)
    cqn = _rms(c[:, :MLA_Q_LORA], qn_ref[...]).astype(BF16)
    ckvn = _rms(c[:, MLA_Q_LORA:MLA_Q_LORA + MLA_KV_LORA], kvn_ref[...]).astype(BF16)
    kr = c[:, 384:512]
    krr = c[:, 512:640]
    cosq = tab_ref[:, 0:128]
    sinq = tab_ref[:, 128:256]
    cosk = tab_ref[:, 256:384]
    sink = tab_ref[:, 384:512]
    k_rope = kr * cosk + krr * sink
    q = _dot(cqn, wq_ref[...])
    qr = _dot(cqn, wqr_ref[...])
    kn = _dot(ckvn, wkv_ref[...])
    for h in range(MLA_HEADS):
        sl = slice(h * LANES, (h + 1) * LANES)
        q_ref[:, sl] = (q[:, sl] * cosq + qr[:, sl] * sinq).astype(BF16)
        k_ref[:, sl] = (kn[:, sl] + k_rope).astype(BF16)
    v_ref[...] = (_dot(ckvn, wv_ref[...]) + vone_ref[...]).astype(BF16)


def _front0(x2, g, a_w_in, a_q_norm, a_w_q_up, a_kv_norm, a_w_kv_up, seq):
    t_rows = x2.shape[0]
    tm = ROW_TILE
    n_seq = seq // tm
    w = a_w_in
    wkr = w[:, 384:416]
    zeros = lambda n: jnp.zeros((D_MODEL, n), F32)
    wkr_full = jnp.concatenate([zeros(64), wkr, zeros(32)], axis=1)
    wkr_rot = jnp.concatenate([zeros(64), -wkr[:, 16:], wkr[:, :16], zeros(32)], axis=1)
    wc = jnp.concatenate([w[:, :384], wkr_full, wkr_rot], axis=1).astype(BF16)
    wn = w[:, 416:].astype(BF16)
    wq3 = a_w_q_up.reshape(MLA_Q_LORA, MLA_HEADS, MLA_NOPE + MLA_ROPE)
    nope, rope = wq3[..., :MLA_NOPE], wq3[..., MLA_NOPE:]
    pad32 = jnp.zeros((MLA_Q_LORA, MLA_HEADS, 32), F32)
    pad64 = jnp.zeros((MLA_Q_LORA, MLA_HEADS, 64), F32)
    wq = jnp.concatenate([nope, rope, pad32], axis=-1).reshape(MLA_Q_LORA, -1).astype(BF16)
    wqr = jnp.concatenate([pad64, -rope[..., 16:], rope[..., :16], pad32],
                          axis=-1).reshape(MLA_Q_LORA, -1).astype(BF16)
    wkv3 = a_w_kv_up.reshape(MLA_KV_LORA, MLA_HEADS, MLA_NOPE + MLA_V)
    knope, vup = wkv3[..., :MLA_NOPE], wkv3[..., MLA_NOPE:]
    kpad = jnp.zeros((MLA_KV_LORA, MLA_HEADS, 64), F32)
    wkv = jnp.concatenate([knope, kpad], axis=-1).reshape(MLA_KV_LORA, -1).astype(BF16)
    vup4 = vup.reshape(MLA_KV_LORA, MLA_HEADS // 2, 2, MLA_V)
    vpad = jnp.zeros((MLA_KV_LORA, MLA_HEADS // 2, 64), F32)
    wv = jnp.concatenate([vup4[:, :, 0], vpad, vpad, vup4[:, :, 1]], axis=-1)
    wv = wv.reshape(MLA_KV_LORA, -1).astype(BF16)
    pair_one = np.zeros((256,), np.float32)
    pair_one[64] = 1.0
    pair_one[128] = 1.0
    vone = jnp.asarray(np.tile(pair_one, MLA_HEADS // 2)[None, :])
    cos, sin = _rope_tables(seq, MLA_ROPE)
    sc = (MLA_NOPE + MLA_ROPE) ** -0.5 * LOG2E
    z16 = jnp.zeros((seq, 32), F32)
    z64 = jnp.zeros((seq, 64), F32)
    tab = jnp.concatenate([
        jnp.full((seq, 64), sc, F32), cos * sc, cos * sc, z16,
        z64, sin * sc, sin * sc, z16,
        z64, cos, cos, z16,
        z64, sin, sin, z16], axis=1)
    out_shapes = [jax.ShapeDtypeStruct((t_rows, 1024), BF16)] * 3 + \
                 [jax.ShapeDtypeStruct((t_rows, NA_WIDTH), BF16)] * 3
    row = lambda n: pl.BlockSpec((tm, n), lambda i: (i, 0))
    return pl.pallas_call(
        _front0_kernel,
        grid=(t_rows // tm,),
        in_specs=[row(D_MODEL), _const_spec((1, D_MODEL)), _const_spec(wc.shape), _const_spec(wn.shape),
                  _const_spec((1, MLA_Q_LORA)), _const_spec((1, MLA_KV_LORA)), _const_spec(wq.shape),
                  _const_spec(wqr.shape), _const_spec(wkv.shape), _const_spec(wv.shape),
                  _const_spec((1, 1024)),
                  pl.BlockSpec((tm, 512), lambda i: (i % n_seq, 0))],
        out_specs=[row(1024), row(1024), row(1024), row(NA_WIDTH), row(NA_WIDTH), row(NA_WIDTH)],
        out_shape=out_shapes,
        compiler_params=_cparams(("parallel",)),
        name="front0",
    )(x2, g[None, :], wc, wn, a_q_norm[None, :], a_kv_norm[None, :], wq, wqr, wkv, wv, vone, tab)


def _pair_normalise(acc_even, acc_odd):
    lane = lax.broadcasted_iota(jnp.int32, acc_even.shape, 1)
    return jnp.where(lane < 64, acc_even / acc_even[:, 64:65], acc_odd / acc_odd[:, 0:1])


def _staggered(units, bufs, scores, weighted_values):
    accs = {}
    pending = None
    for i, unit in enumerate(units):
        s_ref = bufs[i % len(bufs)]
        mrow = scores(unit, s_ref)
        if pending is not None:
            accs[pending[0]] = weighted_values(*pending)
        pending = (unit, s_ref, mrow)
    accs[pending[0]] = weighted_values(*pending)
    return accs


def _mla_kernel(q_ref, k_ref, v_ref, o_ref, s0_ref, s1_ref):
    sub = MLA_SUB_TILE
    seq = k_ref.shape[0]
    n_sub = q_ref.shape[0] // sub
    chunks = [slice(c * MLA_K_CHUNK, (c + 1) * MLA_K_CHUNK) for c in range(seq // MLA_K_CHUNK)]
    head = lambda hh: slice(hh * LANES, (hh + 1) * LANES)

    def scores(unit, s_ref):
        r, hh = unit
        qh = q_ref[r * sub:(r + 1) * sub, head(hh)]
        m = jnp.full((sub, LANES), -jnp.inf, F32)
        for cs in chunks:
            s = _dot_nt(qh, k_ref[cs, head(hh)])
            s_ref[:, cs] = s
            for j in range(MLA_K_CHUNK // LANES):
                m = jnp.maximum(m, s[:, j * LANES:(j + 1) * LANES])
        return jnp.max(m, axis=-1, keepdims=True)

    def weighted_values(unit, s_ref, mrow):
        _, hh = unit
        acc = jnp.zeros((sub, LANES), F32)
        for cs in chunks:
            p = jnp.exp2(s_ref[:, cs] - mrow).astype(BF16)
            acc = acc + _dot(p, v_ref[cs, head(hh)])
        return acc

    units = [(r, hh) for r in range(n_sub) for hh in range(2)]
    accs = _staggered(units, (s0_ref, s1_ref), scores, weighted_values)
    for r in range(n_sub):
        o_ref[r * sub:(r + 1) * sub, :] = _pair_normalise(accs[(r, 0)], accs[(r, 1)]).astype(BF16)


def _mla_attention(q, k, v2, batch, seq):
    tq = MLA_Q_TILE
    nq = seq // tq
    return pl.pallas_call(
        _mla_kernel,
        grid=(batch, MLA_HEADS // 2, nq),
        in_specs=[pl.BlockSpec((tq, 256), lambda b, p, i: (b * nq + i, p)),
                  pl.BlockSpec((seq, 256), lambda b, p, i: (b, p)),
                  pl.BlockSpec((seq, 256), lambda b, p, i: (b, p))],
        out_specs=pl.BlockSpec((tq, LANES), lambda b, p, i: (b * nq + i, p)),
        out_shape=jax.ShapeDtypeStruct((batch * seq, MLA_HEADS * MLA_V), BF16),
        scratch_shapes=[pltpu.VMEM((MLA_SUB_TILE, seq), F32), pltpu.VMEM((MLA_SUB_TILE, seq), F32)],
        compiler_params=_cparams(("parallel", "parallel", "arbitrary")),
        name="mla_attention",
    )(q, k, v2)


def _na_bias_table(rpb, rows):
    kh, kw = NA_WIN_H, NA_WIN_W
    neg = -1e30
    c = np.arange(GRID_W)[:, None]
    kc = np.arange(GRID_W)[None, :]
    cst = np.clip(c - kw // 2, 0, GRID_W - kw)
    col_valid = (kc >= cst) & (kc < cst + kw)
    col_off = kc - c + (kw - 1)
    onehot = ((col_off[None] == np.arange(2 * kw - 1)[:, None, None]) & col_valid[None]).astype(np.float32)
    blocks = jnp.einsum('hdj,jck->hdck', rpb.astype(F32), jnp.asarray(onehot),
                        precision=lax.Precision.HIGHEST)
    blocks = jnp.where(col_valid, blocks, neg)
    masked = jnp.full((NA_HEADS, GRID_W, GRID_W), neg, F32)
    tables = []
    for r0, ws in ((0, 0), (2 * NA_ROWS, 2 * NA_ROWS - kh // 2), (rows - NA_ROWS, rows - NA_KEY_ROWS)):
        q_rows = []
        for ri in range(NA_ROWS):
            r = r0 + ri
            rs = min(max(r - kh // 2, 0), rows - kh)
            k_blocks = []
            for i in range(NA_KEY_ROWS):
                kr = ws + i
                k_blocks.append(blocks[:, kr - r + (kh - 1)] if rs <= kr < rs + kh else masked)
            q_rows.append(jnp.concatenate(k_blocks, axis=-1))
        tables.append(jnp.concatenate(q_rows, axis=1))
    return jnp.stack(tables)


def _na_kernel(q_ref, k_ref, v_ref, b_ref, o_ref, s0_ref, s1_ref):
    g = pl.program_id(1)
    rows = k_ref.shape[0] // GRID_W
    ws = jnp.clip(g * NA_ROWS - NA_WIN_H // 2, 0, rows - NA_KEY_ROWS)
    start = pl.multiple_of(ws * GRID_W, GRID_W)
    nkeys = NA_KEY_ROWS * GRID_W
    nqry = NA_ROWS * GRID_W
    lane = lax.broadcasted_iota(jnp.int32, (1, LANES), 1)
    pair = lambda p: slice(p * LANES, (p + 1) * LANES)
    chunks = [slice(c * NA_K_CHUNK, (c + 1) * NA_K_CHUNK) for c in range(nkeys // NA_K_CHUNK)]

    def scores(unit, s_ref):
        p, hh = unit
        q2 = q_ref[:, pair(p)]
        own = (lane < 64) if hh == 0 else (lane >= 64)
        qm = jnp.where(own, q2, jnp.zeros_like(q2))
        m = jnp.full((nqry, LANES), -jnp.inf, F32)
        for cs in chunks:
            k2 = k_ref[pl.ds(pl.multiple_of(start + cs.start, GRID_W), NA_K_CHUNK), pair(p)]
            s = _dot_nt(qm, k2) + b_ref[0, 2 * p + hh, :, cs]
            s_ref[:, cs] = s
            for j in range(NA_K_CHUNK // LANES):
                m = jnp.maximum(m, s[:, j * LANES:(j + 1) * LANES])
        return jnp.max(m, axis=-1, keepdims=True)

    def weighted_values(unit, s_ref, mrow):
        p, hh = unit
        own = (lane < 64) if hh == 0 else (lane >= 64)
        ones_lane = 64 if hh == 0 else 0
        acc = jnp.zeros((nqry, LANES), F32)
        for cs in chunks:
            v2 = v_ref[pl.ds(pl.multiple_of(start + cs.start, GRID_W), NA_K_CHUNK), pair(p)]
            vm = jnp.where(own, v2, jnp.where(lane == ones_lane, 1.0, 0.0).astype(BF16))
            acc = acc + _dot(jnp.exp2(s_ref[:, cs] - mrow).astype(BF16), vm)
        return acc

    units = [(p, hh) for p in range(NA_HEADS // 2) for hh in range(2)]
    accs = _staggered(units, (s0_ref, s1_ref), scores, weighted_values)
    for p in range(NA_HEADS // 2):
        o_ref[:, pair(p)] = _pair_normalise(accs[(p, 0)], accs[(p, 1)]).astype(BF16)


def _na_attention(nq, nk, nv, rpb, batch, seq):
    rows = seq // GRID_W
    groups = rows // NA_ROWS
    nqry = NA_ROWS * GRID_W
    nkeys = NA_KEY_ROWS * GRID_W
    bias = _na_bias_table(rpb * LOG2E, rows)
    case = lambda g: (g > 0).astype(jnp.int32) + (g == groups - 1).astype(jnp.int32)
    return pl.pallas_call(
        _na_kernel,
        grid=(batch, groups),
        in_specs=[pl.BlockSpec((nqry, NA_WIDTH), lambda b, g: (b * groups + g, 0)),
                  pl.BlockSpec((seq, NA_WIDTH), lambda b, g: (b, 0)),
                  pl.BlockSpec((seq, NA_WIDTH), lambda b, g: (b, 0)),
                  pl.BlockSpec((1, NA_HEADS, nqry, nkeys), lambda b, g: (case(g), 0, 0, 0))],
        out_specs=pl.BlockSpec((nqry, NA_WIDTH), lambda b, g: (b * groups + g, 0)),
        out_shape=jax.ShapeDtypeStruct((batch * seq, NA_WIDTH), BF16),
        scratch_shapes=[pltpu.VMEM((nqry, nkeys), F32), pltpu.VMEM((nqry, nkeys), F32)],
        compiler_params=_cparams(("parallel", "arbitrary")),
        name="na_attention",
    )(nq, nk, nv, bias)


def _gelu_tanh(x):
    return 0.5 * x * (1.0 + jnp.tanh(math.sqrt(2.0 / math.pi) * (x + 0.044715 * (x * x * x))))


def _mix_ffn_kernel(a_ref, ap_ref, an_ref, b_ref, bp_ref, bn_ref, x_ref, xp_ref, xn_ref, wa_ref, wb_ref,
                    gmix_ref, gpre_ref, wg_ref, wu_ref, cw_ref, wd_ref, gpost_ref, o_ref, act_ref, *, n_seq):
    i = pl.program_id(0)
    tm = x_ref.shape[0]
    ext = tm + 2 * HALO
    with_halo = lambda prev, mid, nxt: jnp.concatenate([prev[...], mid[...], nxt[...]], axis=0)
    mix = (_dot(with_halo(ap_ref, a_ref, an_ref), wa_ref[...])
           + _dot(with_halo(bp_ref, b_ref, bn_ref), wb_ref[...]))
    x1 = with_halo(xp_ref, x_ref, xn_ref) + _rms(mix, gmix_ref[...])
    hn = _rms(x1, gpre_ref[...])
    r = lax.broadcasted_iota(jnp.int32, (ext, 1), 0)
    outside = ((r < HALO) & (i % n_seq == 0)) | ((r >= HALO + tm) & (i % n_seq == n_seq - 1))
    h_ext = jnp.where(outside, 0.0, hn).astype(BF16)
    hb = h_ext[HALO:HALO + tm]
    for j in range(D_FF // FFN_CHUNK):
        cs = slice(j * FFN_CHUNK, (j + 1) * FFN_CHUNK)
        gate = _dot(h_ext, wg_ref[:, cs])
        up = _dot(hb, wu_ref[:, cs])
        g_prev = pltpu.roll(gate, 1, 0)[HALO:HALO + tm]
        g_next = pltpu.roll(gate, ext - 1, 0)[HALO:HALO + tm]
        conv = g_prev * cw_ref[0:1, cs] + gate[HALO:HALO + tm] * cw_ref[1:2, cs] + g_next * cw_ref[2:3, cs]
        act_ref[:, cs] = (_gelu_tanh(conv) * up).astype(BF16)
    f = _dot(act_ref[...], wd_ref[...])
    o_ref[...] = x1[HALO:HALO + tm] + _rms(f, gpost_ref[...])


def _mix_ffn(a, b, w_out, x2, gmix, gpre, w_gate, w_up, conv_w, w_down, gpost, seq):
    t_rows = x2.shape[0]
    tm = ROW_TILE
    n_seq = seq // tm
    half = w_out.shape[0] // 2
    tile = lambda n: _halo_specs(tm, t_rows, n)
    return pl.pallas_call(
        functools.partial(_mix_ffn_kernel, n_seq=n_seq),
        grid=(t_rows // tm,),
        in_specs=tile(half) + tile(half) + tile(D_MODEL) + [
            _const_spec((half, D_MODEL)), _const_spec((half, D_MODEL)), _const_spec((1, D_MODEL)),
            _const_spec((1, D_MODEL)), _const_spec((D_MODEL, D_FF)), _const_spec((D_MODEL, D_FF)),
            _const_spec((3, D_FF)), _const_spec((D_FF, D_MODEL)), _const_spec((1, D_MODEL))],
        out_specs=pl.BlockSpec((tm, D_MODEL), lambda i: (i, 0)),
        out_shape=jax.ShapeDtypeStruct((t_rows, D_MODEL), F32),
        scratch_shapes=[pltpu.VMEM((tm, D_FF), BF16)],
        compiler_params=_cparams(("parallel",)),
        name="mix_ffn",
    )(a, a, a, b, b, b, x2, x2, x2, w_out[:half].astype(BF16), w_out[half:].astype(BF16), gmix[None, :],
      gpre[None, :], w_gate.astype(BF16), w_up.astype(BF16), conv_w, w_down.astype(BF16), gpost[None, :])


def _front1_kernel(x_ref, xp_ref, xn_ref, g_ref, w_ref, sc_ref, tab_ref, rq_ref, rkt_ref, rv_ref,
                   rg_ref, u_ref, z0_ref, *, n_seq):
    i = pl.program_id(0)
    tm = x_ref.shape[0]
    ext = tm + 2 * HALO
    g = g_ref[...]
    hn = _rms(x_ref[...], g).astype(BF16)
    r = _dot(hn, w_ref[:, :4 * RET_W])
    cosr = tab_ref[:, 0:128]
    sinr = tab_ref[:, 128:256]
    kscale = RET_DIM ** -0.5
    c = RET_CHUNK
    for h in range(RET_HEADS):
        qs = slice(h * RET_DIM, (h + 1) * RET_DIM)
        ks = slice(RET_W + h * RET_DIM, RET_W + (h + 1) * RET_DIM)
        qh = r[:, qs]
        kh = r[:, ks]
        rq_ref[:, qs] = (qh * cosr + pltpu.roll(qh, RET_DIM // 2, 1) * sinr).astype(BF16)
        krot = (kh * cosr + pltpu.roll(kh, RET_DIM // 2, 1) * sinr) * kscale
        for j in range(tm // c):
            rkt_ref[h, j] = krot[j * c:(j + 1) * c].T.astype(BF16)
    rv_ref[...] = r[:, 2 * RET_W:3 * RET_W].astype(BF16)
    rg_ref[...] = r[:, 3 * RET_W:]
    keep_prev = jnp.where(i % n_seq == 0, 0.0, 1.0)
    keep_next = jnp.where(i % n_seq == n_seq - 1, 0.0, 1.0)
    h_ext = jnp.concatenate([(_rms(xp_ref[...], g) * keep_prev).astype(BF16), hn,
                             (_rms(xn_ref[...], g) * keep_next).astype(BF16)], axis=0)
    hy = _dot(h_ext, w_ref[:, 4 * RET_W:])
    z = (pltpu.roll(hy, 1, 0)[HALO:HALO + tm] * sc_ref[0:1, :] + hy[HALO:HALO + tm] * sc_ref[1:2, :]
         + pltpu.roll(hy, ext - 1, 0)[HALO:HALO + tm] * sc_ref[2:3, :])
    z0_ref[...] = z[:, :HY_WIDTH].astype(BF16)
    u_ref[...] = (z[:, 2 * HY_WIDTH:] * z[:, HY_WIDTH:2 * HY_WIDTH]).astype(BF16)


def _front1(x2, g, c_w_in, short_conv, seq):
    t_rows = x2.shape[0]
    tm = ROW_TILE
    n_seq = seq // tm
    cos, sin = _rope_tables(seq, RET_DIM)
    tab = jnp.concatenate([cos, cos, -sin, sin], axis=1)
    w = c_w_in.astype(BF16)
    row = lambda n: pl.BlockSpec((tm, n), lambda i: (i, 0))
    kt_shape = (RET_HEADS, t_rows // RET_CHUNK, RET_DIM, RET_CHUNK)
    kt_spec = pl.BlockSpec((RET_HEADS, tm // RET_CHUNK, RET_DIM, RET_CHUNK), lambda i: (0, i, 0, 0))
    out_shapes = [jax.ShapeDtypeStruct((t_rows, RET_W), BF16), jax.ShapeDtypeStruct(kt_shape, BF16),
                  jax.ShapeDtypeStruct((t_rows, RET_W), BF16), jax.ShapeDtypeStruct((t_rows, RET_W), F32),
                  jax.ShapeDtypeStruct((t_rows, HY_WIDTH), BF16), jax.ShapeDtypeStruct((t_rows, HY_WIDTH), BF16)]
    return pl.pallas_call(
        functools.partial(_front1_kernel, n_seq=n_seq),
        grid=(t_rows // tm,),
        in_specs=_halo_specs(tm, t_rows, D_MODEL) + [
            _const_spec((1, D_MODEL)), _const_spec(w.shape), _const_spec(short_conv.shape),
            pl.BlockSpec((tm, 256), lambda i: (i % n_seq, 0))],
        out_specs=[row(RET_W), kt_spec, row(RET_W), row(RET_W), row(HY_WIDTH), row(HY_WIDTH)],
        out_shape=out_shapes,
        compiler_params=_cparams(("parallel",)),
        name="front1",
    )(x2, x2, x2, g[None, :], w, short_conv, tab)


def _log_sigmoid(x):
    return jnp.minimum(x, 0.0) - jnp.log1p(jnp.exp(-jnp.abs(x)))


def _ret_kernel(q_ref, kt_ref, v_ref, g_ref, dec_ref, o_ref, sc_ref, kv_ref, st_ref):
    seq = q_ref.shape[0]
    c = RET_CHUNK
    n_chunks = seq // c
    lf = _log_sigmoid(dec_ref[0, 0:1, :])
    lb = _log_sigmoid(dec_ref[0, 1:2, :])
    t = lax.broadcasted_iota(jnp.int32, (c, c), 0).astype(F32)
    s_idx = lax.broadcasted_iota(jnp.int32, (c, c), 1).astype(F32)
    diff = t - s_idx
    dmat = jnp.where(diff >= 0, jnp.exp(jnp.maximum(diff, 0.0) * lf),
                     jnp.exp(jnp.maximum(-diff, 0.0) * lb))
    xi_f = jnp.exp((t + 1.0) * lf)
    xi_b = jnp.exp((c - t) * lb)
    zeta_f = jnp.exp((c - 1.0 - s_idx) * lf)
    zeta_b = jnp.exp(s_idx * lb)
    g_f = jnp.exp(c * lf)
    g_b = jnp.exp(c * lb)

    def chunk_a(n, carry):
        rs = pl.ds(pl.multiple_of(n * c, c), c)
        ktn = kt_ref[0, n]
        sc_ref[n] = (_dot(q_ref[rs, :], ktn) * dmat).astype(BF16)
        ktf = ktn.astype(F32)
        kz = jnp.concatenate([ktf * zeta_f, ktf * zeta_b], axis=0).astype(BF16)
        kv_ref[n] = _dot(kz, v_ref[rs, :])
        return carry

    lax.fori_loop(0, n_chunks, chunk_a, 0, unroll=RET_UNROLL)

    def scan_f(n, st):
        st_ref[n, :, :c] = st.astype(BF16)
        return g_f * st + kv_ref[n, :c, :]

    lax.fori_loop(0, n_chunks, scan_f, jnp.zeros((c, c), F32))

    def scan_b(m, st):
        n = n_chunks - 1 - m
        st_ref[n, :, c:] = st.astype(BF16)
        return g_b * st + kv_ref[n, c:, :]

    lax.fori_loop(0, n_chunks, scan_b, jnp.zeros((c, c), F32))

    def chunk_c(n, carry):
        rs = pl.ds(pl.multiple_of(n * c, c), c)
        cross = _dot(q_ref[rs, :], st_ref[n])
        o = _dot(sc_ref[n], v_ref[rs, :]) + cross[:, :c] * xi_f + cross[:, c:] * xi_b
        o = o * lax.rsqrt(jnp.mean(o * o, axis=-1, keepdims=True) + EPS)
        gate = g_ref[rs, :]
        o_ref[rs, :] = (o * (gate * jax.nn.sigmoid(gate))).astype(BF16)
        return carry

    lax.fori_loop(0, n_chunks, chunk_c, 0, unroll=RET_UNROLL)


def _retention(rq, rkt, rv, rg, decay_fwd, decay_bwd, batch, seq):
    dec = jnp.stack([decay_fwd, decay_bwd], axis=1)
    dec = jnp.broadcast_to(dec[:, :, None], (RET_HEADS, 2, LANES)).astype(F32)
    n_chunks = seq // RET_CHUNK
    blk = pl.BlockSpec((seq, RET_DIM), lambda b, h: (b, h))
    return pl.pallas_call(
        _ret_kernel,
        grid=(batch, RET_HEADS),
        in_specs=[blk, pl.BlockSpec((1, n_chunks, RET_DIM, RET_CHUNK), lambda b, h: (h, b, 0, 0)),
                  blk, blk, pl.BlockSpec((1, 2, LANES), lambda b, h: (h, 0, 0))],
        out_specs=blk,
        out_shape=jax.ShapeDtypeStruct((batch * seq, RET_W), BF16),
        scratch_shapes=[pltpu.VMEM((n_chunks, RET_CHUNK, RET_CHUNK), BF16),
                        pltpu.VMEM((n_chunks, 2 * RET_DIM, RET_DIM), F32),
                        pltpu.VMEM((n_chunks, RET_DIM, 2 * RET_DIM), BF16)],
        compiler_params=_cparams(("parallel", "parallel")),
        name="retention",
    )(rq, rkt, rv, rg, dec)


def _fft_tables():
    n = FFT_N1 * FFT_N2
    n2 = np.arange(FFT_N2)[:, None, None]
    k1 = np.arange(FFT_N1)[None, :, None]
    n1 = np.arange(FFT_N1 // 2)[None, None, :]
    ang = 2.0 * np.pi * ((k1 * (FFT_N2 * n1 + n2)) % n) / n
    c1, s1 = np.cos(ang), np.sin(ang)
    f1 = np.concatenate([c1, -s1], axis=1)
    f1c = np.concatenate([np.concatenate([c1, s1], axis=2),
                          np.concatenate([-s1, c1], axis=2)], axis=1)
    c3, s3 = c1.transpose(0, 2, 1), s1.transpose(0, 2, 1)
    g3c = np.concatenate([np.concatenate([c3, -s3], axis=2),
                          np.concatenate([s3, c3], axis=2)], axis=1)
    a = np.arange(FFT_N2)
    ang2 = 2.0 * np.pi * ((a[:, None] * a[None, :]) % FFT_N2) / FFT_N2
    cc, ss = np.cos(ang2), np.sin(ang2)
    g2 = np.block([[cc, ss], [-ss, cc]])
    g2i = np.block([[cc, -ss], [ss, cc]])
    return dict(f1=f1, f1c=f1c, g2=g2, g2i=g2i, g3c=g3c)


def _hymlp_kernel(z_ref, w1_ref, b1_ref, w2_ref, b2_ref, w3_ref, b3_ref, fr_ref, h_ref):
    hp = lax.Precision.HIGHEST
    fr = fr_ref[...]

    def dense(h, w_ref, b_ref):
        return jnp.sin(fr * (jnp.dot(h, w_ref[...], precision=hp, preferred_element_type=F32) + b_ref[...]))

    h_ref[...] = dense(dense(dense(z_ref[...], w1_ref, b1_ref), w2_ref, b2_ref), w3_ref, b3_ref)


def _hyfilt_kernel(h_ref, w4f_ref, w4b_ref, dl_ref, f1_ref, g2_ref, kf_ref,
                   tf_ref, tb_ref, af_ref, ab_ref):
    seq = h_ref.shape[0]
    hp = lax.Precision.HIGHEST
    blk = FFT_N2
    rows = HY_MLP_ROWS
    inv_n = 1.0 / (FFT_N1 * FFT_N2)
    dl = jnp.abs(dl_ref[...])

    def fill(i, carry):
        row0 = pl.multiple_of(i * rows, rows)
        t_idx = row0 + lax.broadcasted_iota(jnp.int32, (rows, LANES), 0)
        window = jnp.exp(-(t_idx.astype(F32) / (seq - 1)) * dl)
        h = h_ref[pl.ds(row0, rows), :]
        for w4_ref, t_ref, backward in ((w4f_ref, tf_ref, False), (w4b_ref, tb_ref, True)):
            filt = jnp.dot(h, w4_ref[...], precision=hp, preferred_element_type=F32) * window
            if backward:
                filt = jnp.where(t_idx == 0, 0.0, filt)
            for q in range(rows // blk):
                dst = pl.multiple_of((i * (rows // blk) + q) * PITCH_T, 8)
                t_ref[pl.ds(dst, blk), :] = filt[q * blk:(q + 1) * blk]
        return carry

    lax.fori_loop(0, seq // rows, fill, 0)

    def stage1(n2, carry):
        ts = pl.ds(n2, FFT_N1 // 2, stride=PITCH_T)
        r = _dot(f1_ref[n2], jnp.concatenate([tf_ref[ts, :], tb_ref[ts, :]], axis=1).astype(BF16))
        for a_ref, ls in ((af_ref, slice(0, LANES)), (ab_ref, slice(LANES, 2 * LANES))):
            a_ref[pl.ds(n2, FFT_N1, stride=PITCH_A), :] = r[:FFT_N1, ls]
            a_ref[pl.ds(FFT_N2 + n2, FFT_N1, stride=PITCH_A), :] = r[FFT_N1:, ls]
        return carry

    lax.fori_loop(0, FFT_N2, stage1, 0, unroll=FFT_UNROLL)

    def stage2(j, carry):
        k1s = [j * 2, j * 2 + 1]
        blocks = [pl.ds(pl.multiple_of(k1 * PITCH_A, 8), 2 * FFT_N2) for k1 in k1s]
        w = jnp.concatenate([ref[rs, :] for rs in blocks for ref in (af_ref, ab_ref)], axis=1)
        y = _dot(g2_ref[...], w.astype(BF16)) * inv_n
        for g, k1 in enumerate(k1s):
            yf = y[:, (2 * g) * LANES:(2 * g + 1) * LANES]
            yb = y[:, (2 * g + 1) * LANES:(2 * g + 2) * LANES]
            kf_ref[k1] = jnp.concatenate([yf[:FFT_N2] + yb[:FFT_N2], yf[FFT_N2:] - yb[FFT_N2:]], axis=0)
        return carry

    lax.fori_loop(0, FFT_N1 // 2, stage2, 0, unroll=2)


def _hyena_filter_spectrum(w1, b1, w2, b2, w3, b3, w4, freq, seq, tables):
    f1, g2 = tables["f1"], tables["g2"]
    t = jnp.arange(seq, dtype=F32) / (seq - 1)
    bands = (HY_EMB_DIM - 1) // 2
    w = 2.0 * math.pi * jnp.arange(seq, dtype=F32) / seq
    f = jnp.linspace(1e-4, bands - 1, bands, dtype=F32)
    fw = f[None, :] * w[:, None]
    z = jnp.concatenate([t[:, None], jnp.cos(fw), -jnp.sin(fw)], axis=-1)
    z = jnp.pad(z, ((0, 0), (0, LANES - HY_EMB_DIM)))
    w1p = jnp.pad(w1, ((0, LANES - HY_EMB_DIM), (0, 0)))
    max_decay = math.log(HY_TARGET) / HY_FAST_DECAY
    min_decay = math.log(HY_TARGET) / HY_SLOW_DECAY
    deltas = jnp.linspace(min_decay, max_decay, HY_WIDTH, dtype=F32)[None, :]
    f1 = jnp.asarray(f1, F32).astype(BF16)
    g2 = jnp.asarray(g2, F32).astype(BF16)
    n_ct = HY_WIDTH // LANES
    hid = HY_FILTER_HIDDEN
    rows_t = (FFT_N1 // 2) * PITCH_T
    rows_a = FFT_N1 * PITCH_A
    h3 = pl.pallas_call(
        _hymlp_kernel,
        grid=(seq // HY_MLP_ROWS,),
        in_specs=[pl.BlockSpec((HY_MLP_ROWS, LANES), lambda i: (i, 0)), _const_spec((LANES, hid)),
                  _const_spec((1, hid)), _const_spec((hid, hid)), _const_spec((1, hid)),
                  _const_spec((hid, hid)), _const_spec((1, hid)), _const_spec((1, hid))],
        out_specs=pl.BlockSpec((HY_MLP_ROWS, hid), lambda i: (i, 0)),
        out_shape=jax.ShapeDtypeStruct((seq, hid), F32),
        compiler_params=_cparams(("parallel",)),
        name="hyena_mlp",
    )(z, w1p, b1[None, :], w2, b2[None, :], w3, b3[None, :], freq[None, :])
    return pl.pallas_call(
        _hyfilt_kernel,
        grid=(n_ct,),
        in_specs=[_const_spec((seq, hid)),
                  pl.BlockSpec((hid, LANES), lambda c: (0, c)),
                  pl.BlockSpec((hid, LANES), lambda c: (0, n_ct + c)),
                  pl.BlockSpec((1, LANES), lambda c: (0, c)),
                  _const_spec(f1.shape), _const_spec(g2.shape)],
        out_specs=pl.BlockSpec((FFT_N1, 2 * FFT_N2, LANES), lambda c: (0, 0, c)),
        out_shape=jax.ShapeDtypeStruct((FFT_N1, 2 * FFT_N2, HY_WIDTH), F32),
        scratch_shapes=[pltpu.VMEM((rows_t, LANES), F32), pltpu.VMEM((rows_t, LANES), F32),
                        pltpu.VMEM((rows_a, LANES), F32), pltpu.VMEM((rows_a, LANES), F32)],
        compiler_params=_cparams(("arbitrary",)),
        name="hyena_filter",
    )(h3, w4, w4, deltas, f1, g2)


def _hyconv_kernel(u_ref, z0_ref, bias_ref, kf_ref, f1_ref, g2_ref, g2i_ref, g3_ref, o_ref,
                   ua_ref, ub_ref, a_ref, b_ref):
    blk = FFT_N2
    half = FFT_N1 // 2
    scratch = (ua_ref, ub_ref)

    def fill(n1, carry):
        src = pl.ds(pl.multiple_of(n1 * blk, blk), blk)
        dst = pl.ds(pl.multiple_of(n1 * PITCH_T, 8), blk)
        for s, t_ref in enumerate(scratch):
            t_ref[dst, :] = u_ref[s, src, :].astype(F32)
        return carry

    lax.fori_loop(0, half, fill, 0, unroll=FFT_UNROLL)

    def stage1(n2, carry):
        xs = jnp.concatenate([ua_ref[pl.ds(n2, half, stride=PITCH_T), :],
                              ub_ref[pl.ds(n2, half, stride=PITCH_T), :]], axis=0).astype(BF16)
        r = _dot(f1_ref[n2], xs)
        a_ref[pl.ds(n2, FFT_N1, stride=PITCH_A), :] = r[:FFT_N1]
        a_ref[pl.ds(FFT_N2 + n2, FFT_N1, stride=PITCH_A), :] = r[FFT_N1:]
        return carry

    lax.fori_loop(0, FFT_N2, stage1, 0, unroll=FFT_UNROLL)

    def group_blocks(j):
        k1s = [j * FFT_GROUP + g for g in range(FFT_GROUP)]
        return k1s, [pl.ds(pl.multiple_of(k1 * PITCH_A, 8), 2 * FFT_N2) for k1 in k1s]

    def stage2_fwd(j, carry):
        k1s, blocks = group_blocks(j)
        w = jnp.concatenate([a_ref[rs, :].astype(BF16) for rs in blocks], axis=1)
        y = _dot(g2_ref[...], w)
        kf = jnp.concatenate([kf_ref[k1] for k1 in k1s], axis=1)
        yr, yi = y[:FFT_N2], y[FFT_N2:]
        kr, ki = kf[:FFT_N2], kf[FFT_N2:]
        z = jnp.concatenate([yr * kr - yi * ki, yr * ki + yi * kr], axis=0)
        for g, rs in enumerate(blocks):
            a_ref[rs, :] = z[:, g * LANES:(g + 1) * LANES]
        return carry

    lax.fori_loop(0, FFT_N1 // FFT_GROUP, stage2_fwd, 0, unroll=2)

    def stage2_inv(j, carry):
        k1s, blocks = group_blocks(j)
        z = jnp.concatenate([a_ref[rs, :].astype(BF16) for rs in blocks], axis=1)
        bv = _dot(g2i_ref[...], z)
        for g, k1 in enumerate(k1s):
            ls = slice(g * LANES, (g + 1) * LANES)
            b_ref[pl.ds(k1, FFT_N2, stride=PITCH_B), :] = bv[:FFT_N2, ls]
            b_ref[pl.ds(FFT_N1 + k1, FFT_N2, stride=PITCH_B), :] = bv[FFT_N2:, ls]
        return carry

    lax.fori_loop(0, FFT_N1 // FFT_GROUP, stage2_inv, 0, unroll=2)
    bias = bias_ref[...]

    def stage3(n2, carry):
        rs = pl.ds(pl.multiple_of(n2 * PITCH_B, 8), 2 * FFT_N1)
        y = _dot(g3_ref[n2], b_ref[rs, :].astype(BF16))
        ts = pl.ds(n2, half, stride=PITCH_T)
        ua_ref[ts, :] = y[:half] + ua_ref[ts, :] * bias
        ub_ref[ts, :] = y[half:] + ub_ref[ts, :] * bias
        return carry

    lax.fori_loop(0, FFT_N2, stage3, 0, unroll=FFT_UNROLL)

    def finish(n1, carry):
        src = pl.ds(pl.multiple_of(n1 * PITCH_T, 8), blk)
        dst = pl.ds(pl.multiple_of(n1 * blk, blk), blk)
        for s, t_ref in enumerate(scratch):
            o_ref[s, dst, :] = (t_ref[src, :] * z0_ref[s, dst, :].astype(F32)).astype(BF16)
        return carry

    lax.fori_loop(0, half, finish, 0, unroll=FFT_UNROLL)


def _hyena_conv(u, z0, hy_bias, kf, batch, seq, tables):
    n_ct = HY_WIDTH // LANES
    rows_t = (FFT_N1 // 2) * PITCH_T
    pair = pl.BlockSpec((2, seq, LANES), lambda c, b: (b, 0, c))
    const = lambda name: jnp.asarray(tables[name], F32).astype(BF16)
    f1c, g2, g2i, g3c = const("f1c"), const("g2"), const("g2i"), const("g3c")
    y = pl.pallas_call(
        _hyconv_kernel,
        grid=(n_ct, batch // 2),
        in_specs=[pair, pair, pl.BlockSpec((1, LANES), lambda c, b: (0, c)),
                  pl.BlockSpec((FFT_N1, 2 * FFT_N2, LANES), lambda c, b: (0, 0, c),
                               pipeline_mode=pl.Buffered(1)),
                  _const_spec(f1c.shape), _const_spec(g2.shape), _const_spec(g2i.shape),
                  _const_spec(g3c.shape)],
        out_specs=pair,
        out_shape=jax.ShapeDtypeStruct((batch, seq, HY_WIDTH), BF16),
        scratch_shapes=[pltpu.VMEM((rows_t, LANES), F32), pltpu.VMEM((rows_t, LANES), F32),
                        pltpu.VMEM((FFT_N1 * PITCH_A, LANES), F32),
                        pltpu.VMEM((FFT_N2 * PITCH_B, LANES), F32)],
        compiler_params=_cparams(("arbitrary", "arbitrary")),
        name="hyena_conv",
    )(u.reshape(batch, seq, HY_WIDTH), z0.reshape(batch, seq, HY_WIDTH), hy_bias[None, :], kf,
      f1c, g2, g2i, g3c)
    return y.reshape(batch * seq, HY_WIDTH)


def kernel(x, mix_pre_norm, mix_post_norm, ffn_pre_norm, ffn_post_norm, ffn_w_gate, ffn_w_up, ffn_conv, ffn_w_down, a_w_in, a_q_norm, a_w_q_up, a_kv_norm, a_w_kv_up, a_rpb, a_w_out, c_w_in, c_decay_fwd, c_decay_bwd, c_short_conv, c_filt_w1, c_filt_b1, c_filt_w2, c_filt_b2, c_filt_w3, c_filt_b3, c_filt_w4, c_filt_freq, c_hy_bias, c_w_out):
    batch, seq, d = x.shape
    assert d == D_MODEL and seq * 2 == FFT_N1 * FFT_N2 and seq % ROW_TILE == 0 and batch % 2 == 0
    x2 = x.reshape(batch * seq, d)

    q, k, v2, nq, nk, nv = _front0(x2, mix_pre_norm[0], a_w_in[0], a_q_norm[0], a_w_q_up[0],
                                   a_kv_norm[0], a_w_kv_up[0], seq)
    a = _mla_attention(q, k, v2, batch, seq)
    b = _na_attention(nq, nk, nv, a_rpb[0], batch, seq)
    x2 = _mix_ffn(a, b, a_w_out[0], x2, mix_post_norm[0], ffn_pre_norm[0], ffn_w_gate[0], ffn_w_up[0],
                  ffn_conv[0], ffn_w_down[0], ffn_post_norm[0], seq)

    rq, rkt, rv, rg, u, z0 = _front1(x2, mix_pre_norm[1], c_w_in[0], c_short_conv[0], seq)
    c = _retention(rq, rkt, rv, rg, c_decay_fwd[0], c_decay_bwd[0], batch, seq)
    tables = _fft_tables()
    kf = _hyena_filter_spectrum(c_filt_w1[0], c_filt_b1[0], c_filt_w2[0], c_filt_b2[0], c_filt_w3[0],
                                c_filt_b3[0], c_filt_w4[0], c_filt_freq[0], seq, tables)
    dd = _hyena_conv(u, z0, c_hy_bias[0], kf, batch, seq, tables)
    x2 = _mix_ffn(c, dd, c_w_out[0], x2, mix_post_norm[1], ffn_pre_norm[1], ffn_w_gate[1], ffn_w_up[1],
                  ffn_conv[1], ffn_w_down[1], ffn_post_norm[1], seq)
    return x2.reshape(batch, seq, d)
```
